```python
import math
import jax, jax.numpy as jnp
from jax import lax
import numpy as np

D_MODEL = 1024
BATCH = 8
SEQ = 4096
DEPTH = 1

SSD_HEADS = 16
SSD_HEAD_DIM = 64
SSD_D_INNER = SSD_HEADS * SSD_HEAD_DIM
SSD_GROUPS = 4
SSD_HPG = SSD_HEADS // SSD_GROUPS
SSD_D_STATE = 128
SSD_CONV = 4
SSD_CONV_DIM = SSD_D_INNER + 2 * SSD_GROUPS * SSD_D_STATE
DT_MIN = 1e-3
DT_MAX = 1e-1
RET_HEADS = 4
RET_QK_DIM = 128
RET_V_DIM = 256
RET_D_QK = RET_HEADS * RET_QK_DIM
RET_D_V = RET_HEADS * RET_V_DIM
ROPE_BASE = 10000.0
CHUNK = 128
FFN_HIDDEN = ((8 * D_MODEL + 3 * 256 - 1) // (3 * 256)) * 256
NORM_EPS = 1e-6
IN_SIZES = (SSD_D_INNER, SSD_CONV_DIM, SSD_HEADS, RET_D_QK, RET_D_QK, RET_D_V, RET_D_V, D_MODEL, D_MODEL)
D_IN_PROJ = sum(IN_SIZES)

kernel_name = 'hybrid_ssd_retention_adaln_block'


def _split(t, sizes):
    idx = np.cumsum(sizes)[:-1].tolist()
    return jnp.split(t, idx, axis=-1)


def rmsnorm(x, w):
    xf = x.astype(jnp.float32)
    y = xf * lax.rsqrt(jnp.mean(xf * xf, axis=-1, keepdims=True) + NORM_EPS)
    return (y * w.astype(jnp.float32)).astype(x.dtype)


def to_chunks(t):
    b, s = t.shape[:2]
    t = t.reshape((b, s // CHUNK, CHUNK) + t.shape[2:])
    return jnp.moveaxis(t, 1, 0)


def from_chunks(t):
    nc, b, l = t.shape[:3]
    return jnp.moveaxis(t, 0, 1).reshape((b, nc * l) + t.shape[3:])


def causal_depthwise_conv(x, w, b):
    y = lax.conv_general_dilated(x, w[:, None, :], window_strides=(1,), padding=[(SSD_CONV - 1, 0)],
                                 dimension_numbers=('NWC', 'WIO', 'NWC'), feature_group_count=x.shape[-1])
    return y + b


def rotary(t, positions):
    half = t.shape[-1] // 2
    inv_freq = jnp.power(ROPE_BASE, -jnp.arange(half, dtype=jnp.float32) / half)
    ang = positions.astype(jnp.float32)[:, :, None, None] * inv_freq
    cos, sin = jnp.cos(ang), jnp.sin(ang)
    t1, t2 = t[..., :half], t[..., half:]
    return jnp.concatenate([t1 * cos - t2 * sin, t2 * cos + t1 * sin], axis=-1)


def ssd_mixer(xbc, z, dt_raw, conv_w, conv_b, dt_bias, A_log, D_skip, norm_w):
    bsz, seqlen, _ = xbc.shape
    out_dtype = xbc.dtype
    f32 = jnp.float32
    G, R, P, N = SSD_GROUPS, SSD_HPG, SSD_HEAD_DIM, SSD_D_STATE
    xbc = jax.nn.silu(causal_depthwise_conv(xbc, conv_w, conv_b)).astype(f32)
    xs, b_in, c_in = _split(xbc, (SSD_D_INNER, G * N, G * N))
    xs = xs.reshape(bsz, seqlen, G, R, P)
    b_in = b_in.reshape(bsz, seqlen, G, N)
    c_in = c_in.reshape(bsz, seqlen, G, N)
    dt = jax.nn.softplus(dt_raw.astype(f32) + dt_bias.astype(f32)).reshape(bsz, seqlen, G, R)
    A = -jnp.exp(A_log.astype(f32)).reshape(G, R)
    a = dt * A
    xdt = xs * dt[..., None]
    causal = jnp.tril(jnp.ones((CHUNK, CHUNK), dtype=bool))[None, :, :, None, None]

    def step(state, inp):
        xdt_c, b_c, c_c, a_c = inp
        acum = jnp.cumsum(a_c, axis=1)
        seg = acum[:, :, None] - acum[:, None, :]
        decay = jnp.where(causal, jnp.exp(jnp.where(causal, seg, 0.0)), 0.0)
        cb = jnp.einsum('btgn,bsgn->btsg', c_c, b_c)
        y = jnp.einsum('btsg,btsgr,bsgrp->btgrp', cb, decay, xdt_c)
        y = y + jnp.einsum('btgn,bgrpn->btgrp', c_c, state) * jnp.exp(acum)[..., None]
        a_last = acum[:, -1]
        w_end = jnp.exp(a_last[:, None] - acum)
        state = state * jnp.exp(a_last)[..., None, None] + jnp.einsum('bsgn,bsgrp->bgrpn', b_c, xdt_c * w_end[..., None])
        return state, y

    state0 = jnp.zeros((bsz, G, R, P, N), f32)
    _, ys = lax.scan(step, state0, (to_chunks(xdt), to_chunks(b_in), to_chunks(c_in), to_chunks(a)))
    y = from_chunks(ys) + D_skip.astype(f32).reshape(G, R)[..., None] * xs
    y = y.reshape(bsz, seqlen, SSD_D_INNER) * jax.nn.silu(z.astype(f32))
    y = y.reshape(bsz, seqlen, G, SSD_D_INNER // G)
    y = y * lax.rsqrt(jnp.mean(y * y, axis=-1, keepdims=True) + NORM_EPS)
    y = y.reshape(bsz, seqlen, SSD_D_INNER) * norm_w.astype(f32)
    return y.astype(out_dtype)


def retention_mixer(q, k, v, g, positions, norm_w):
    bsz, seqlen, _ = q.shape
    out_dtype = q.dtype
    f32 = jnp.float32
    H = RET_HEADS
    q = rotary(q.astype(f32).reshape(bsz, seqlen, H, RET_QK_DIM), positions)
    k = rotary(k.astype(f32).reshape(bsz, seqlen, H, RET_QK_DIM), positions) * (RET_QK_DIM ** -0.5)
    v = v.astype(f32).reshape(bsz, seqlen, H, RET_V_DIM)
    log_gamma = jnp.log1p(-jnp.exp2(-5.0 - jnp.arange(H, dtype=f32)))
    idx = jnp.arange(CHUNK, dtype=f32)
    diff = idx[:, None] - idx[None, :]
    dmask = jnp.where(diff >= 0, jnp.exp(jnp.maximum(diff, 0.0)[None] * log_gamma[:, None, None]), 0.0)
    q_decay = jnp.exp((idx + 1.0)[:, None] * log_gamma)
    k_decay = jnp.exp((CHUNK - 1.0 - idx)[:, None] * log_gamma)
    chunk_decay = jnp.exp(CHUNK * log_gamma)

    def step(state, inp):
        q_c, k_c, v_c = inp
        scores = jnp.einsum('bthd,bshd->bhts', q_c, k_c) * dmask
        y = jnp.einsum('bhts,bshv->bthv', scores, v_c)
        y = y + jnp.einsum('bthd,bhdv->bthv', q_c, state) * q_decay[:, :, None]
        state = state * chunk_decay[:, None, None] + jnp.einsum('bshd,bshv->bhdv', k_c * k_decay[:, :, None], v_c)
        return state, y

    state0 = jnp.zeros((bsz, H, RET_QK_DIM, RET_V_DIM), f32)
    _, ys = lax.scan(step, state0, (to_chunks(q), to_chunks(k), to_chunks(v)))
    y = from_chunks(ys)
    y = y * lax.rsqrt(jnp.mean(y * y, axis=-1, keepdims=True) + NORM_EPS)
    y = y * norm_w.astype(f32).reshape(H, RET_V_DIM)
    y = y.reshape(bsz, seqlen, RET_D_V) * jax.nn.silu(g.astype(f32))
    return y.astype(out_dtype)


def setup_inputs(seed: int = 0) -> dict:
    key = jax.random.key(seed)
    ks = jax.random.split(key, 24)
    f32 = jnp.float32

    def nrm(k, shape, scale):
        return jax.random.normal(k, shape, f32) * scale

    x = nrm(ks[0], (BATCH, SEQ, D_MODEL), 1.0)
    c = nrm(ks[1], (BATCH, D_MODEL), 1.0)
    offset = jax.random.randint(ks[2], (BATCH, 1), 0, SEQ, dtype=jnp.int32)
    positions = offset + jnp.arange(SEQ, dtype=jnp.int32)[None, :]
    w_ada = nrm(ks[3], (DEPTH, D_MODEL, 6 * D_MODEL), D_MODEL ** -0.5)
    b_ada = nrm(ks[4], (DEPTH, 6 * D_MODEL), 0.01)
    norm_mix_w = 1.0 + nrm(ks[5], (DEPTH, D_MODEL), 0.02)
    w_in = nrm(ks[6], (DEPTH, D_MODEL, D_IN_PROJ), D_MODEL ** -0.5)
    ssd_conv_w = nrm(ks[7], (DEPTH, SSD_CONV, SSD_CONV_DIM), SSD_CONV ** -0.5)
    ssd_conv_b = nrm(ks[8], (DEPTH, SSD_CONV_DIM), 0.01)
    dt0 = jnp.exp(jax.random.uniform(ks[9], (DEPTH, SSD_HEADS), f32, math.log(DT_MIN), math.log(DT_MAX)))
    ssd_dt_bias = dt0 + jnp.log(-jnp.expm1(-dt0))
    ssd_A_log = jnp.log(jax.random.uniform(ks[10], (DEPTH, SSD_HEADS), f32, 1.0, 16.0))
    ssd_D = 1.0 + nrm(ks[11], (DEPTH, SSD_HEADS), 0.1)
    ssd_norm_w = 1.0 + nrm(ks[12], (DEPTH, SSD_D_INNER), 0.02)
    ret_norm_w = 1.0 + nrm(ks[13], (DEPTH, RET_D_V), 0.02)
    w_branch_ssd = nrm(ks[14], (DEPTH, SSD_D_INNER, D_MODEL), SSD_D_INNER ** -0.5)
    w_branch_ret = nrm(ks[15], (DEPTH, RET_D_V, D_MODEL), RET_D_V ** -0.5)
    w_out = nrm(ks[16], (DEPTH, D_MODEL, D_MODEL), D_MODEL ** -0.5)
    norm_ffn_w = 1.0 + nrm(ks[17], (DEPTH, D_MODEL), 0.02)
    w_gate_up = nrm(ks[18], (DEPTH, D_MODEL, 2 * FFN_HIDDEN), D_MODEL ** -0.5)
    w_down = nrm(ks[19], (DEPTH, FFN_HIDDEN, D_MODEL), FFN_HIDDEN ** -0.5)
    norm_final_w = 1.0 + nrm(ks[20], (D_MODEL,), 0.02)
    return {'x': x, 'c': c, 'positions': positions, 'w_ada': w_ada, 'b_ada': b_ada,
            'norm_mix_w': norm_mix_w, 'w_in': w_in, 'ssd_conv_w': ssd_conv_w, 'ssd_conv_b': ssd_conv_b,
            'ssd_dt_bias': ssd_dt_bias, 'ssd_A_log': ssd_A_log, 'ssd_D': ssd_D, 'ssd_norm_w': ssd_norm_w,
            'ret_norm_w': ret_norm_w, 'w_branch_ssd': w_branch_ssd, 'w_branch_ret': w_branch_ret,
            'w_out': w_out, 'norm_ffn_w': norm_ffn_w, 'w_gate_up': w_gate_up, 'w_down': w_down,
            'norm_final_w': norm_final_w}


def reference(x, c, positions, w_ada, b_ada, norm_mix_w, w_in, ssd_conv_w, ssd_conv_b, ssd_dt_bias,
              ssd_A_log, ssd_D, ssd_norm_w, ret_norm_w, w_branch_ssd, w_branch_ret, w_out, norm_ffn_w,
              w_gate_up, w_down, norm_final_w):
    cond = jax.nn.silu(c)
    for l in range(DEPTH):
        mod = cond @ w_ada[l] + b_ada[l]
        shift_m, scale_m, gate_m, shift_f, scale_f, gate_f = jnp.split(mod[:, None, :], 6, axis=-1)

        h = rmsnorm(x, norm_mix_w[l]) * (1.0 + scale_m) + shift_m
        proj = h @ w_in[l]
        z, xbc, dt_raw, q, k, v, g, gate_a, gate_b = _split(proj, IN_SIZES)
        y_ssd = ssd_mixer(xbc, z, dt_raw, ssd_conv_w[l], ssd_conv_b[l], ssd_dt_bias[l], ssd_A_log[l],
                          ssd_D[l], ssd_norm_w[l])
        y_ret = retention_mixer(q, k, v, g, positions, ret_norm_w[l])
        merged = jax.nn.sigmoid(gate_a) * (y_ssd @ w_branch_ssd[l]) + jax.nn.sigmoid(gate_b) * (y_ret @ w_branch_ret[l])
        x = x + gate_m * (merged @ w_out[l])

        h = rmsnorm(x, norm_ffn_w[l]) * (1.0 + scale_f) + shift_f
        gu = h @ w_gate_up[l]
        gate, up = jnp.split(gu, 2, axis=-1)
        x = x + gate_f * ((jax.nn.silu(gate) * up) @ w_down[l])
    return rmsnorm(x, norm_final_w)
```

```python
import functools
import math

import jax
import jax.numpy as jnp
import numpy as np
from jax import lax
from jax.experimental import pallas as pl
from jax.experimental.pallas import tpu as pltpu

F32 = jnp.float32
BF16 = jnp.bfloat16

D_MODEL = 1024
SSD_HEADS = 16
SSD_HEAD_DIM = 64
SSD_D_INNER = SSD_HEADS * SSD_HEAD_DIM
SSD_GROUPS = 4
SSD_HPG = SSD_HEADS // SSD_GROUPS
SSD_D_STATE = 128
SSD_CONV = 4
SSD_BC = SSD_GROUPS * SSD_D_STATE
SSD_CONV_DIM = SSD_D_INNER + 2 * SSD_BC
RET_HEADS = 4
RET_QK_DIM = 128
RET_V_DIM = 256
RET_D_QK = RET_HEADS * RET_QK_DIM
RET_D_V = RET_HEADS * RET_V_DIM
ROPE_BASE = 10000.0
CHUNK = 128
FFN_HIDDEN = 2816
NORM_EPS = 1e-6
IN_SIZES = (SSD_D_INNER, SSD_CONV_DIM, SSD_HEADS, RET_D_QK, RET_D_QK, RET_D_V, RET_D_V, D_MODEL, D_MODEL)

LANES = 128
CONV_PAD = 8
PROJ_W = 8192
COL_XBC, COL_Z, COL_V, COL_G, COL_GA, COL_GB, COL_Q, COL_K = 0, 4, 6, 8, 10, 12, 14, 15

VMEM_LIMIT = 56 * 1024 * 1024


def _silu(v):
    return v * (1.0 / (1.0 + jnp.exp(-v)))


def _sigmoid(v):
    return 1.0 / (1.0 + jnp.exp(-v))


def _dot(a, b):
    return jnp.dot(a, b, preferred_element_type=F32)


def _dot_nt(a, b):
    return lax.dot_general(a, b, (((1,), (1,)), ((), ())), preferred_element_type=F32)


def _dot_tn(a, b):
    return lax.dot_general(a, b, (((0,), (0,)), ((), ())), preferred_element_type=F32)


def _split3(v):
    hi = v.astype(BF16)
    r1 = v - hi.astype(F32)
    mid = r1.astype(BF16)
    lo = (r1 - mid.astype(F32)).astype(BF16)
    return hi, mid, lo


def _dot3_left(sel, v):
    hi, mid, lo = _split3(v)
    return _dot(sel, hi) + _dot(sel, mid) + _dot(sel, lo)


def _dot3_right(v, sel):
    hi, mid, lo = _split3(v)
    return _dot(hi, sel) + _dot(mid, sel) + _dot(lo, sel)


def _mod_kernel(c_ref, w_ref, b_ref, o_ref):
    cond = _silu(c_ref[...])
    o_ref[...] = jnp.dot(cond, w_ref[...], preferred_element_type=F32,
                         precision=lax.Precision.HIGHEST) + b_ref[...]


def _mod_call(c, w_ada, b_ada):
    bsz, d = c.shape
    n = w_ada.shape[1]
    tn = 1024
    return pl.pallas_call(
        _mod_kernel,
        grid=(n // tn,),
        in_specs=[pl.BlockSpec((bsz, d), lambda j: (0, 0)),
                  pl.BlockSpec((d, tn), lambda j: (0, j)),
                  pl.BlockSpec((1, tn), lambda j: (0, j))],
        out_specs=pl.BlockSpec((bsz, tn), lambda j: (0, j)),
        out_shape=jax.ShapeDtypeStruct((bsz, n), F32),
        name="mod",
    )(c, w_ada, b_ada.reshape(1, n))


def _inproj_kernel(x_ref, shift_ref, scale_ref, nw_ref, w_ref, wdt_ref, proj_ref, dt_ref, h_ref):
    @pl.when(pl.program_id(1) == 0)
    def _():
        x = x_ref[...]
        y = x * lax.rsqrt(jnp.mean(x * x, axis=-1, keepdims=True) + NORM_EPS)
        h = (y * nw_ref[...]) * (1.0 + scale_ref[0]) + shift_ref[0]
        hb = h.astype(BF16)
        h_ref[...] = hb
        dt_ref[...] = _dot(hb, wdt_ref[...])

    proj_ref[...] = _dot(h_ref[...], w_ref[...]).astype(BF16)


def _inproj_call(x2, mod3, norm_w, w_main, w_dt, seq, tm, tn):
    t, d = x2.shape
    tiles_per_seq = seq // tm
    return pl.pallas_call(
        _inproj_kernel,
        grid=(t // tm, PROJ_W // tn),
        in_specs=[pl.BlockSpec((tm, d), lambda i, j: (i, 0)),
                  pl.BlockSpec((1, 1, d), lambda i, j: (i // tiles_per_seq, 0, 0)),
                  pl.BlockSpec((1, 1, d), lambda i, j: (i // tiles_per_seq, 0, 1)),
                  pl.BlockSpec((1, d), lambda i, j: (0, 0)),
                  pl.BlockSpec((d, tn), lambda i, j: (0, j)),
                  pl.BlockSpec((d, LANES), lambda i, j: (0, 0))],
        out_specs=[pl.BlockSpec((tm, tn), lambda i, j: (i, j)),
                   pl.BlockSpec((tm, LANES), lambda i, j: (i, 0))],
        out_shape=[jax.ShapeDtypeStruct((t, PROJ_W), BF16),
                   jax.ShapeDtypeStruct((t, LANES), F32)],
        scratch_shapes=[pltpu.VMEM((tm, d), BF16)],
        compiler_params=pltpu.CompilerParams(
            dimension_semantics=("arbitrary", "arbitrary"), vmem_limit_bytes=VMEM_LIMIT),
        name="in_proj",
    )(x2, mod3, mod3, norm_w, w_main, w_dt)


def _mixer_kernel(xbc_ref, z_ref, v_ref, g_ref, q_ref, k_ref, dt_ref, pos_ref,
                  convw_ref, convb_ref, dtb_ref, alog_ref, dexp_ref, snw_ref, rnw_ref,
                  tri_ref, hexp_ref, invf_ref, dmask_ref, qdec_ref, kdec_ref, cdec_ref,
                  yssd_ref, yret_ref,
                  xpad_ref, sstate_ref, rstate_ref):
    L = CHUNK
    P = SSD_HEAD_DIM
    GP = SSD_HPG * P

    @pl.when(pl.program_id(1) == 0)
    def _():
        xpad_ref[0:CONV_PAD, :] = jnp.zeros((CONV_PAD, SSD_CONV_DIM), F32)
        sstate_ref[...] = jnp.zeros_like(sstate_ref)
        rstate_ref[...] = jnp.zeros_like(rstate_ref)

    row = lax.broadcasted_iota(jnp.int32, (L, L), 0)
    col = lax.broadcasted_iota(jnp.int32, (L, L), 1)
    causal = row >= col

    xpad_ref[CONV_PAD:CONV_PAD + L, :] = xbc_ref[...].astype(F32)
    acc = convb_ref[...]
    for j in range(SSD_CONV):
        off = CONV_PAD - (SSD_CONV - 1) + j
        acc = acc + convw_ref[j:j + 1, :] * xpad_ref[off:off + L, :]
    xpad_ref[0:CONV_PAD, :] = xpad_ref[L:L + CONV_PAD, :]
    xc = _silu(acc)
    xs = xc[:, :SSD_D_INNER]
    b_in = xc[:, SSD_D_INNER:SSD_D_INNER + SSD_BC].astype(BF16)
    c_in = xc[:, SSD_D_INNER + SSD_BC:].astype(BF16)

    dtr = dt_ref[...] + dtb_ref[...]
    dt = jnp.maximum(dtr, 0.0) + jnp.log1p(jnp.exp(-jnp.abs(dtr)))
    a = dt * (-jnp.exp(alog_ref[...]))
    acum = _dot3_left(tri_ref[...], a)
    acum_t = acum.T
    a_last = acum[L - 1:L, :]
    hexp = hexp_ref[...]
    dt_x = _dot3_right(dt, hexp)
    eacum_x = _dot3_right(jnp.exp(acum), hexp)
    wend_x = _dot3_right(jnp.exp(a_last - acum), hexp)
    elast_x = eacum_x[L - 1:L, :]
    xdt = xs * dt_x
    xw = (xdt * wend_x).astype(BF16)
    xdt_b = xdt.astype(BF16)
    lane = lax.broadcasted_iota(jnp.int32, (L, GP), 1)
    zs = _silu(z_ref[...].astype(F32))
    dskip = dexp_ref[...] * xs

    for g in range(SSD_GROUPS):
        cg = c_in[:, g * SSD_D_STATE:(g + 1) * SSD_D_STATE]
        bg = b_in[:, g * SSD_D_STATE:(g + 1) * SSD_D_STATE]
        cb = _dot_nt(cg, bg)
        xg = xdt_b[:, g * GP:(g + 1) * GP]
        y = _dot(cg, sstate_ref[g].astype(BF16)) * eacum_x[:, g * GP:(g + 1) * GP]
        for r in range(SSD_HPG):
            h = g * SSD_HPG + r
            seg = acum[:, h:h + 1] - acum_t[h:h + 1, :]
            decay = jnp.where(causal, jnp.exp(jnp.where(causal, seg, 0.0)), 0.0)
            m = (cb * decay).astype(BF16)
            xh = jnp.where((lane >= r * P) & (lane < (r + 1) * P), xg, jnp.zeros_like(xg))
            y = y + _dot(m, xh)
        sstate_ref[g] = (sstate_ref[g] * elast_x[:, g * GP:(g + 1) * GP]
                         + _dot_tn(bg, xw[:, g * GP:(g + 1) * GP]))
        y = (y + dskip[:, g * GP:(g + 1) * GP]) * zs[:, g * GP:(g + 1) * GP]
        y = y * lax.rsqrt(jnp.mean(y * y, axis=-1, keepdims=True) + NORM_EPS)
        yssd_ref[:, g * GP:(g + 1) * GP] = (y * snw_ref[:, g * GP:(g + 1) * GP]).astype(BF16)

    half = RET_QK_DIM // 2
    ang_t = invf_ref[...] * pos_ref[0]
    cos_t = jnp.cos(ang_t)
    sin_t = jnp.sin(ang_t)
    cos2 = jnp.concatenate([cos_t, cos_t], axis=0).T
    sin2 = jnp.concatenate([-sin_t, sin_t], axis=0).T
    gs = _silu(g_ref[...].astype(F32))
    for h in range(RET_HEADS):
        qh = q_ref[:, h * RET_QK_DIM:(h + 1) * RET_QK_DIM].astype(F32)
        kh = k_ref[:, h * RET_QK_DIM:(h + 1) * RET_QK_DIM].astype(F32)
        qr = qh * cos2 + pltpu.roll(qh, half, 1) * sin2
        kr = (kh * cos2 + pltpu.roll(kh, half, 1) * sin2) * (RET_QK_DIM ** -0.5)
        qb = qr.astype(BF16)
        vh = v_ref[:, h * RET_V_DIM:(h + 1) * RET_V_DIM]
        vs = slice(h * RET_V_DIM, (h + 1) * RET_V_DIM)
        scores = _dot_nt(qb, kr.astype(BF16)) * dmask_ref[h]
        y = _dot(scores.astype(BF16), vh)
        y = y + _dot(qb, rstate_ref[:, vs].astype(BF16)) * qdec_ref[:, vs]
        kd = (kr * kdec_ref[:, h * RET_QK_DIM:(h + 1) * RET_QK_DIM]).astype(BF16)
        rstate_ref[:, vs] = rstate_ref[:, vs] * cdec_ref[:, vs] + _dot_tn(kd, vh)
        y = y * lax.rsqrt(jnp.mean(y * y, axis=-1, keepdims=True) + NORM_EPS)
        yret_ref[:, vs] = (y * rnw_ref[:, vs] * gs[:, vs]).astype(BF16)


def _mixer_consts():
    L = CHUNK
    idx = np.arange(L)
    tri = (idx[:, None] >= idx[None, :]).astype(np.float32)
    hexp = np.zeros((LANES, SSD_D_INNER), np.float32)
    for h in range(SSD_HEADS):
        hexp[h, h * SSD_HEAD_DIM:(h + 1) * SSD_HEAD_DIM] = 1.0
    return jnp.asarray(tri, BF16), jnp.asarray(hexp, BF16)


def _retention_tables():
    L = CHUNK
    H = RET_HEADS
    log_gamma = jnp.log1p(-jnp.exp2(-5.0 - jnp.arange(H, dtype=F32)))
    idx = jnp.arange(L, dtype=F32)
    diff = idx[:, None] - idx[None, :]
    dmask = jnp.where(diff >= 0, jnp.exp(jnp.maximum(diff, 0.0)[None] * log_gamma[:, None, None]), 0.0)
    q_decay = jnp.exp((idx + 1.0)[:, None] * log_gamma)
    k_decay = jnp.exp((L - 1.0 - idx)[:, None] * log_gamma)
    chunk_decay = jnp.exp(L * log_gamma)
    qdec = jnp.repeat(q_decay, RET_V_DIM, axis=1)
    kdec = jnp.repeat(k_decay, RET_QK_DIM, axis=1)
    cdec = jnp.repeat(chunk_decay, RET_V_DIM)[None, :]
    half = RET_QK_DIM // 2
    inv_freq = jnp.power(ROPE_BASE, -jnp.arange(half, dtype=F32) / half)
    invf = jnp.broadcast_to(inv_freq[:, None], (half, LANES))
    return dmask, qdec, kdec, cdec, invf


def _mixer_call(proj, dt_raw, pos3, bsz, seq, conv_w, conv_b, dt_bias, a_log, d_skip, ssd_norm_w, ret_norm_w):
    t = proj.shape[0]
    nc = seq // CHUNK
    tri, hexp = _mixer_consts()
    dmask, qdec, kdec, cdec, invf = _retention_tables()

    def pad_heads(v):
        return jnp.zeros((1, LANES), F32).at[0, :SSD_HEADS].set(v.astype(F32))

    def tok(width, colblk):
        return pl.BlockSpec((CHUNK, width), lambda b, c: (b * nc + c, colblk))

    def full(shape):
        nd = len(shape)
        return pl.BlockSpec(shape, lambda b, c: (0,) * nd)

    consts = [conv_w.astype(F32), conv_b.reshape(1, -1).astype(F32), pad_heads(dt_bias), pad_heads(a_log),
              jnp.repeat(d_skip.astype(F32), SSD_HEAD_DIM)[None, :], ssd_norm_w.reshape(1, -1).astype(F32),
              ret_norm_w.reshape(1, -1).astype(F32), tri, hexp, invf, dmask, qdec, kdec, cdec]
    in_specs = [tok(SSD_CONV_DIM, COL_XBC * 512 // SSD_CONV_DIM),
                tok(SSD_D_INNER, COL_Z * 512 // SSD_D_INNER),
                tok(RET_D_V, COL_V * 512 // RET_D_V),
                tok(RET_D_V, COL_G * 512 // RET_D_V),
                tok(RET_D_QK, COL_Q * 512 // RET_D_QK),
                tok(RET_D_QK, COL_K * 512 // RET_D_QK),
                tok(LANES, 0),
                pl.BlockSpec((1, 1, CHUNK), lambda b, c: (b * nc + c, 0, 0))]
    in_specs += [full(v.shape) for v in consts]
    return pl.pallas_call(
        _mixer_kernel,
        grid=(bsz, nc),
        in_specs=in_specs,
        out_specs=[pl.BlockSpec((CHUNK, SSD_D_INNER), lambda b, c: (b * nc + c, 0)),
                   pl.BlockSpec((CHUNK, RET_D_V), lambda b, c: (b * nc + c, 0))],
        out_shape=[jax.ShapeDtypeStruct((t, SSD_D_INNER), BF16),
                   jax.ShapeDtypeStruct((t, RET_D_V), BF16)],
        scratch_shapes=[pltpu.VMEM((CONV_PAD + CHUNK, SSD_CONV_DIM), F32),
                        pltpu.VMEM((SSD_GROUPS, SSD_D_STATE, SSD_HPG * SSD_HEAD_DIM), F32),
                        pltpu.VMEM((RET_QK_DIM, RET_D_V), F32)],
        compiler_params=pltpu.CompilerParams(
            dimension_semantics=("arbitrary", "arbitrary"), vmem_limit_bytes=VMEM_LIMIT),
        name="mixers",
    )(proj, proj, proj, proj, proj, proj, dt_raw, pos3, *consts)


def _merge_kernel(ys_ref, yr_ref, ga_ref, gb_ref, x_ref, gm_ref, shf_ref, scf_ref, nw_ref,
                  wbs_ref, wbr_ref, wo_ref, x1_ref, h2_ref):
    a = _dot(ys_ref[...], wbs_ref[...])
    b = _dot(yr_ref[...], wbr_ref[...])
    merged = _sigmoid(ga_ref[...].astype(F32)) * a + _sigmoid(gb_ref[...].astype(F32)) * b
    x1 = x_ref[...] + gm_ref[0] * _dot(merged.astype(BF16), wo_ref[...])
    x1_ref[...] = x1
    y = x1 * lax.rsqrt(jnp.mean(x1 * x1, axis=-1, keepdims=True) + NORM_EPS)
    h2_ref[...] = ((y * nw_ref[...]) * (1.0 + scf_ref[0]) + shf_ref[0]).astype(BF16)


def _merge_call(yssd, yret, proj, x2, mod3, norm_ffn_w, wbs, wbr, wo, seq, tm):
    t, d = x2.shape
    tps = seq // tm

    def tok(colblk=0):
        return pl.BlockSpec((tm, d), lambda i: (i, colblk))

    def modrow(k):
        return pl.BlockSpec((1, 1, d), lambda i: (i // tps, 0, k))

    def full(shape):
        return pl.BlockSpec(shape, lambda i: (0, 0))

    return pl.pallas_call(
        _merge_kernel,
        grid=(t // tm,),
        in_specs=[tok(), tok(), tok(COL_GA * 512 // d), tok(COL_GB * 512 // d), tok(),
                  modrow(2), modrow(3), modrow(4), full((1, d)),
                  full(wbs.shape), full(wbr.shape), full(wo.shape)],
        out_specs=[tok(), tok()],
        out_shape=[jax.ShapeDtypeStruct((t, d), F32), jax.ShapeDtypeStruct((t, d), BF16)],
        compiler_params=pltpu.CompilerParams(
            dimension_semantics=("arbitrary",), vmem_limit_bytes=VMEM_LIMIT),
        name="merge",
    )(yssd, yret, proj, proj, x2, mod3, mod3, mod3, norm_ffn_w, wbs, wbr, wo)


def _ffn_kernel(h2_ref, x1_ref, gf_ref, nw_ref, wg_ref, wu_ref, wd_ref, o_ref, acc_ref):
    j = pl.program_id(1)
    h2 = h2_ref[...]
    gate = _dot(h2, wg_ref[...])
    up = _dot(h2, wu_ref[...])
    part = _dot((_silu(gate) * up).astype(BF16), wd_ref[...])

    @pl.when(j == 0)
    def _():
        acc_ref[...] = part

    @pl.when(j > 0)
    def _():
        acc_ref[...] += part

    @pl.when(j == pl.num_programs(1) - 1)
    def _():
        x2 = x1_ref[...] + gf_ref[0] * acc_ref[...]
        y = x2 * lax.rsqrt(jnp.mean(x2 * x2, axis=-1, keepdims=True) + NORM_EPS)
        o_ref[...] = y * nw_ref[...]


def _ffn_call(h2, x1, mod3, norm_final_w, wg, wu, wd, seq, tm, th):
    t, d = x1.shape
    tps = seq // tm
    hid = wg.shape[1]
    return pl.pallas_call(
        _ffn_kernel,
        grid=(t // tm, hid // th),
        in_specs=[pl.BlockSpec((tm, d), lambda i, j: (i, 0)),
                  pl.BlockSpec((tm, d), lambda i, j: (i, 0)),
                  pl.BlockSpec((1, 1, d), lambda i, j: (i // tps, 0, 5)),
                  pl.BlockSpec((1, d), lambda i, j: (0, 0)),
                  pl.BlockSpec((d, th), lambda i, j: (0, j)),
                  pl.BlockSpec((d, th), lambda i, j: (0, j)),
                  pl.BlockSpec((th, d), lambda i, j: (j, 0))],
        out_specs=pl.BlockSpec((tm, d), lambda i, j: (i, 0)),
        out_shape=jax.ShapeDtypeStruct((t, d), F32),
        scratch_shapes=[pltpu.VMEM((tm, d), F32)],
        compiler_params=pltpu.CompilerParams(
            dimension_semantics=("arbitrary", "arbitrary"), vmem_limit_bytes=VMEM_LIMIT),
        name="ffn",
    )(h2, x1, mod3, norm_final_w, wg, wu, wd)


def _pick_tile(seq, want):
    tm = min(want, seq)
    assert seq % tm == 0 and tm % CHUNK == 0
    return tm


def kernel(x, c, positions, w_ada, b_ada, norm_mix_w, w_in, ssd_conv_w, ssd_conv_b, ssd_dt_bias, ssd_A_log, ssd_D, ssd_norm_w, ret_norm_w, w_branch_ssd, w_branch_ret, w_out, norm_ffn_w, w_gate_up, w_down, norm_final_w):
    bsz, seq, d = x.shape
    assert d == D_MODEL and seq % CHUNK == 0
    assert w_ada.shape[0] == 1, "single-layer block"
    t = bsz * seq
    x2 = x.reshape(t, d)

    mod = _mod_call(c, w_ada[0], b_ada[0])
    mod3 = mod.reshape(bsz, 1, 6 * d)

    w_z, w_xbc, w_dt, w_q, w_k, w_v, w_g, w_ga, w_gb = jnp.split(w_in[0], np.cumsum(IN_SIZES)[:-1].tolist(), axis=1)
    w_main = jnp.concatenate([w_xbc, w_z, w_v, w_g, w_ga, w_gb, w_q, w_k], axis=1).astype(BF16)
    w_dtp = jnp.zeros((d, LANES), BF16).at[:, :SSD_HEADS].set(w_dt.astype(BF16))

    proj, dt_raw = _inproj_call(x2, mod3, norm_mix_w.reshape(1, d), w_main, w_dtp, seq,
                                tm=_pick_tile(seq, 1024), tn=1024)

    pos3 = positions.astype(F32).reshape(t // CHUNK, 1, CHUNK)
    yssd, yret = _mixer_call(proj, dt_raw, pos3, bsz, seq, ssd_conv_w[0], ssd_conv_b[0], ssd_dt_bias[0],
                             ssd_A_log[0], ssd_D[0], ssd_norm_w[0], ret_norm_w[0])

    x1, h2 = _merge_call(yssd, yret, proj, x2, mod3, norm_ffn_w.reshape(1, d),
                         w_branch_ssd[0].astype(BF16), w_branch_ret[0].astype(BF16), w_out[0].astype(BF16),
                         seq, tm=_pick_tile(seq, 512))

    wg = w_gate_up[0][:, :FFN_HIDDEN].astype(BF16)
    wu = w_gate_up[0][:, FFN_HIDDEN:].astype(BF16)
    out = _ffn_call(h2, x1, mod3, norm_final_w.reshape(1, d), wg, wu, w_down[0].astype(BF16),
                    seq, tm=_pick_tile(seq, 512), th=FFN_HIDDEN // 2)
    return out.reshape(bsz, seq, d)
```

```python
import math

import jax
import jax.numpy as jnp
import numpy as np
from jax import lax
from jax.experimental import pallas as pl
from jax.experimental.pallas import tpu as pltpu

F32 = jnp.float32
BF16 = jnp.bfloat16

D_MODEL = 1024
SSD_HEADS = 16
SSD_HEAD_DIM = 64
SSD_D_INNER = SSD_HEADS * SSD_HEAD_DIM
SSD_GROUPS = 4
SSD_HPG = SSD_HEADS // SSD_GROUPS
SSD_D_STATE = 128
SSD_CONV = 4
SSD_BC = SSD_GROUPS * SSD_D_STATE
SSD_CONV_DIM = SSD_D_INNER + 2 * SSD_BC
RET_HEADS = 4
RET_QK_DIM = 128
RET_V_DIM = 256
RET_D_QK = RET_HEADS * RET_QK_DIM
RET_D_V = RET_HEADS * RET_V_DIM
ROPE_BASE = 10000.0
CHUNK = 128
FFN_HIDDEN = 2816
NORM_EPS = 1e-6
IN_SIZES = (SSD_D_INNER, SSD_CONV_DIM, SSD_HEADS, RET_D_QK, RET_D_QK, RET_D_V, RET_D_V, D_MODEL, D_MODEL)

LANES = 128
PROJ_W = 8192
PROJ_BLK = 1024
BLK_XBC, BLK_Z, BLK_V, BLK_G, BLK_GA, BLK_GB, BLK_QK = 0, 2, 3, 4, 5, 6, 7
SILU_BLOCKS = (BLK_Z, BLK_G)
SIGMOID_BLOCKS = (BLK_GA, BLK_GB)
LOG2E = math.log2(math.e)
MASKED_LOG2 = -1e30

VMEM_LIMIT = 56 * 1024 * 1024
TOKEN_TILE = 512


def _silu(v):
    return v * (1.0 / (1.0 + jnp.exp(-v)))


def _sigmoid(v):
    return 1.0 / (1.0 + jnp.exp(-v))


def _dot(a, b):
    return jnp.dot(a, b, preferred_element_type=F32)


def _dot_nt(a, b):
    return lax.dot_general(a, b, (((1,), (1,)), ((), ())), preferred_element_type=F32)


def _dot_tn(a, b):
    return lax.dot_general(a, b, (((0,), (0,)), ((), ())), preferred_element_type=F32)


def _split3(v):
    hi = v.astype(BF16)
    r1 = v - hi.astype(F32)
    mid = r1.astype(BF16)
    lo = (r1 - mid.astype(F32)).astype(BF16)
    return hi, mid, lo


def _resident(shape):
    nd = len(shape)
    return pl.BlockSpec(shape, lambda *_: (0,) * nd, pipeline_mode=pl.Buffered(1))


def _mod_kernel(c_ref, w_ref, b_ref, o_ref):
    cond = _silu(c_ref[...])
    o_ref[...] = jnp.dot(cond, w_ref[...], preferred_element_type=F32,
                         precision=lax.Precision.HIGHEST) + b_ref[...]


def _mod_call(c, w_ada, b_ada):
    bsz, d = c.shape
    n = w_ada.shape[1]
    tn = 1024
    return pl.pallas_call(
        _mod_kernel,
        grid=(n // tn,),
        in_specs=[pl.BlockSpec((bsz, d), lambda j: (0, 0)),
                  pl.BlockSpec((d, tn), lambda j: (0, j)),
                  pl.BlockSpec((1, tn), lambda j: (0, j))],
        out_specs=pl.BlockSpec((bsz, tn), lambda j: (0, j)),
        out_shape=jax.ShapeDtypeStruct((bsz, n), F32),
        name="mod",
    )(c, w_ada, b_ada.reshape(1, n))


def _inproj_kernel(x_ref, shift_ref, scale_ref, nw_ref, w_ref, wdt_ref, proj_ref, dt_ref):
    x = x_ref[...]
    y = x * lax.rsqrt(jnp.mean(x * x, axis=-1, keepdims=True) + NORM_EPS)
    hb = ((y * nw_ref[...]) * (1.0 + scale_ref[0]) + shift_ref[0]).astype(BF16)
    dt_ref[...] = _dot(hb, wdt_ref[...])
    for blk in range(PROJ_W // PROJ_BLK):
        cols = slice(blk * PROJ_BLK, (blk + 1) * PROJ_BLK)
        r = _dot(hb, w_ref[:, cols])
        if blk in SILU_BLOCKS:
            r = _silu(r)
        elif blk in SIGMOID_BLOCKS:
            r = _sigmoid(r)
        proj_ref[:, cols] = r.astype(BF16)


def _inproj_call(x2, mod3, norm_w, w_main, w_dt, seq, tm):
    t, d = x2.shape
    tps = seq // tm
    return pl.pallas_call(
        _inproj_kernel,
        grid=(t // tm,),
        in_specs=[pl.BlockSpec((tm, d), lambda i: (i, 0)),
                  pl.BlockSpec((1, 1, d), lambda i: (i // tps, 0, 0)),
                  pl.BlockSpec((1, 1, d), lambda i: (i // tps, 0, 1)),
                  _resident((1, d)),
                  _resident(w_main.shape),
                  _resident(w_dt.shape)],
        out_specs=[pl.BlockSpec((tm, PROJ_W), lambda i: (i, 0)),
                   pl.BlockSpec((tm, LANES), lambda i: (i, 0))],
        out_shape=[jax.ShapeDtypeStruct((t, PROJ_W), BF16),
                   jax.ShapeDtypeStruct((t, LANES), F32)],
        compiler_params=pltpu.CompilerParams(
            dimension_semantics=("arbitrary",), vmem_limit_bytes=VMEM_LIMIT),
        name="in_proj",
    )(x2, mod3, mod3, norm_w, w_main, w_dt)


def _mixer_kernel(xbc_ref, zs_ref, v_ref, gs_ref, qk_ref, dt_ref, pos_ref,
                  convw_ref, convb_ref, dtb_ref, alog_ref, dexp_ref, snw_ref, rnw_ref,
                  tri_ref, hexp_ref, shift_ref, invf_ref, dmask_ref, qdec_ref, kdec_ref, cdec_ref,
                  yssd_ref, yret_ref,
                  xcat_ref, xbd_ref, sstate_ref, rstate_ref):
    L = CHUNK
    P = SSD_HEAD_DIM
    GP = SSD_HPG * P

    @pl.when(pl.program_id(1) == 0)
    def _():
        xcat_ref[0:L, :] = jnp.zeros((L, SSD_CONV_DIM), BF16)
        xbd_ref[...] = jnp.zeros_like(xbd_ref)
        sstate_ref[...] = jnp.zeros_like(sstate_ref)
        rstate_ref[...] = jnp.zeros_like(rstate_ref)

    row = lax.broadcasted_iota(jnp.int32, (L, L), 0)
    col = lax.broadcasted_iota(jnp.int32, (L, L), 1)
    causal = row >= col

    xcat_ref[L:2 * L, :] = xbc_ref[...]
    taps = _dot(shift_ref[...], xcat_ref[...])
    xcat_ref[0:L, :] = xbc_ref[...]
    acc = convb_ref[...]
    for j in range(SSD_CONV):
        acc = acc + convw_ref[j:j + 1, :] * taps[j * L:(j + 1) * L, :]
    xc = _silu(acc)
    xs = xc[:, :SSD_D_INNER]
    b_in = xc[:, SSD_D_INNER:SSD_D_INNER + SSD_BC].astype(BF16)
    c_in = xc[:, SSD_D_INNER + SSD_BC:].astype(BF16)

    dtr = dt_ref[...] + dtb_ref[...]
    dt = jnp.maximum(dtr, 0.0) + jnp.log1p(jnp.exp(-jnp.abs(dtr)))
    a2 = dt * (-LOG2E * jnp.exp(alog_ref[...]))
    a_hi, a_mid, a_lo = _split3(a2)
    acum3 = _dot(tri_ref[...], jnp.concatenate([a_hi, a_mid, a_lo], axis=1))
    acum = acum3[:, :LANES] + acum3[:, LANES:2 * LANES] + acum3[:, 2 * LANES:]
    acum_t = acum.T
    a_last = acum[L - 1:L, :]
    per_head = jnp.concatenate([dt, jnp.exp2(acum), jnp.exp2(a_last - acum)], axis=0)
    spread = _dot(jnp.concatenate(_split3(per_head), axis=1), hexp_ref[...])
    dt_x, eacum_x, wend_x = spread[:L], spread[L:2 * L], spread[2 * L:]
    elast_x = eacum_x[L - 1:L, :]
    xdt = xs * dt_x
    xw = (xdt * wend_x).astype(BF16)
    xdt_b = xdt.astype(BF16)
    dskip = dexp_ref[...] * xs

    for g in range(SSD_GROUPS):
        gcols = slice(g * GP, (g + 1) * GP)
        cg = c_in[:, g * SSD_D_STATE:(g + 1) * SSD_D_STATE]
        bg = b_in[:, g * SSD_D_STATE:(g + 1) * SSD_D_STATE]
        cb = _dot_nt(cg, bg)
        ms = []
        for r in range(SSD_HPG):
            h = g * SSD_HPG + r
            seg = acum[:, h:h + 1] - acum_t[h:h + 1, :]
            decay = jnp.exp2(jnp.where(causal, seg, MASKED_LOG2))
            ms.append((cb * decay).astype(BF16))
            xbd_ref[g, r * L:(r + 1) * L, r * P:(r + 1) * P] = xdt_b[:, g * GP + r * P:g * GP + (r + 1) * P]
        y = _dot(jnp.concatenate(ms, axis=1), xbd_ref[g])
        y = y + _dot(cg, sstate_ref[g].astype(BF16)) * eacum_x[:, gcols]
        sstate_ref[g] = sstate_ref[g] * elast_x[:, gcols] + _dot_tn(bg, xw[:, gcols])
        y = (y + dskip[:, gcols]) * zs_ref[:, gcols].astype(F32)
        y = y * lax.rsqrt(jnp.mean(y * y, axis=-1, keepdims=True) + NORM_EPS)
        yssd_ref[:, gcols] = (y * snw_ref[:, gcols]).astype(BF16)

    half = RET_QK_DIM // 2
    ang_t = invf_ref[...] * pos_ref[0]
    cos_t = jnp.cos(ang_t)
    sin_t = jnp.sin(ang_t)
    cos2 = jnp.concatenate([cos_t, cos_t], axis=0).T
    sin2 = jnp.concatenate([-sin_t, sin_t], axis=0).T
    kscale = RET_QK_DIM ** -0.5
    cos2k = cos2 * kscale
    sin2k = sin2 * kscale
    for h in range(RET_HEADS):
        qh = qk_ref[:, h * RET_QK_DIM:(h + 1) * RET_QK_DIM].astype(F32)
        kh = qk_ref[:, RET_D_QK + h * RET_QK_DIM:RET_D_QK + (h + 1) * RET_QK_DIM].astype(F32)
        qr = qh * cos2 + pltpu.roll(qh, half, 1) * sin2
        kr = kh * cos2k + pltpu.roll(kh, half, 1) * sin2k
        qb = qr.astype(BF16)
        vs = slice(h * RET_V_DIM, (h + 1) * RET_V_DIM)
        vh = v_ref[:, vs]
        scores = _dot_nt(qb, kr.astype(BF16)) * dmask_ref[h]
        y = _dot(scores.astype(BF16), vh)
        y = y + _dot(qb, rstate_ref[:, vs].astype(BF16)) * qdec_ref[:, vs]
        kd = (kr * kdec_ref[:, h * RET_QK_DIM:(h + 1) * RET_QK_DIM]).astype(BF16)
        rstate_ref[:, vs] = rstate_ref[:, vs] * cdec_ref[:, vs] + _dot_tn(kd, vh)
        y = y * lax.rsqrt(jnp.mean(y * y, axis=-1, keepdims=True) + NORM_EPS)
        yret_ref[:, vs] = (y * rnw_ref[:, vs] * gs_ref[:, vs].astype(F32)).astype(BF16)


def _mixer_consts():
    L = CHUNK
    idx = np.arange(L)
    tri = (idx[:, None] >= idx[None, :]).astype(np.float32)
    hexp = np.zeros((3 * LANES, SSD_D_INNER), np.float32)
    for piece in range(3):
        for h in range(SSD_HEADS):
            hexp[piece * LANES + h, h * SSD_HEAD_DIM:(h + 1) * SSD_HEAD_DIM] = 1.0
    shift = np.zeros((SSD_CONV * L, 2 * L), np.float32)
    for j in range(SSD_CONV):
        shift[j * L + idx, L + idx - (SSD_CONV - 1) + j] = 1.0
    return jnp.asarray(tri, BF16), jnp.asarray(hexp, BF16), jnp.asarray(shift, BF16)


def _retention_tables():
    L = CHUNK
    H = RET_HEADS
    log_gamma = jnp.log1p(-jnp.exp2(-5.0 - jnp.arange(H, dtype=F32)))
    idx = jnp.arange(L, dtype=F32)
    diff = idx[:, None] - idx[None, :]
    dmask = jnp.where(diff >= 0, jnp.exp(jnp.maximum(diff, 0.0)[None] * log_gamma[:, None, None]), 0.0)
    q_decay = jnp.exp((idx + 1.0)[:, None] * log_gamma)
    k_decay = jnp.exp((L - 1.0 - idx)[:, None] * log_gamma)
    chunk_decay = jnp.exp(L * log_gamma)
    qdec = jnp.repeat(q_decay, RET_V_DIM, axis=1)
    kdec = jnp.repeat(k_decay, RET_QK_DIM, axis=1)
    cdec = jnp.repeat(chunk_decay, RET_V_DIM)[None, :]
    half = RET_QK_DIM // 2
    inv_freq = jnp.power(ROPE_BASE, -jnp.arange(half, dtype=F32) / half)
    invf = jnp.broadcast_to(inv_freq[:, None], (half, LANES))
    return dmask, qdec, kdec, cdec, invf


def _mixer_call(proj, dt_raw, pos3, bsz, seq, conv_w, conv_b, dt_bias, a_log, d_skip, ssd_norm_w, ret_norm_w):
    t = proj.shape[0]
    nc = seq // CHUNK
    tri, hexp, shift = _mixer_consts()
    dmask, qdec, kdec, cdec, invf = _retention_tables()

    def pad_heads(v):
        return jnp.zeros((1, LANES), F32).at[0, :SSD_HEADS].set(v.astype(F32))

    def tok(width, blk):
        return pl.BlockSpec((CHUNK, width), lambda b, c: (b * nc + c, blk * PROJ_BLK // width))

    consts = [conv_w.astype(F32), conv_b.reshape(1, -1).astype(F32), pad_heads(dt_bias), pad_heads(a_log),
              jnp.repeat(d_skip.astype(F32), SSD_HEAD_DIM)[None, :], ssd_norm_w.reshape(1, -1).astype(F32),
              ret_norm_w.reshape(1, -1).astype(F32), tri, hexp, shift, invf, dmask, qdec, kdec, cdec]
    in_specs = [tok(SSD_CONV_DIM, BLK_XBC), tok(SSD_D_INNER, BLK_Z), tok(RET_D_V, BLK_V), tok(RET_D_V, BLK_G),
                tok(2 * RET_D_QK, BLK_QK), tok(LANES, 0),
                pl.BlockSpec((1, 1, CHUNK), lambda b, c: (b * nc + c, 0, 0))]
    in_specs += [_resident(v.shape) for v in consts]
    return pl.pallas_call(
        _mixer_kernel,
        grid=(bsz, nc),
        in_specs=in_specs,
        out_specs=[pl.BlockSpec((CHUNK, SSD_D_INNER), lambda b, c: (b * nc + c, 0)),
                   pl.BlockSpec((CHUNK, RET_D_V), lambda b, c: (b * nc + c, 0))],
        out_shape=[jax.ShapeDtypeStruct((t, SSD_D_INNER), BF16),
                   jax.ShapeDtypeStruct((t, RET_D_V), BF16)],
        scratch_shapes=[pltpu.VMEM((2 * CHUNK, SSD_CONV_DIM), BF16),
                        pltpu.VMEM((SSD_GROUPS, SSD_HPG * CHUNK, SSD_HPG * SSD_HEAD_DIM), BF16),
                        pltpu.VMEM((SSD_GROUPS, SSD_D_STATE, SSD_HPG * SSD_HEAD_DIM), F32),
                        pltpu.VMEM((RET_QK_DIM, RET_D_V), F32)],
        compiler_params=pltpu.CompilerParams(
            dimension_semantics=("arbitrary", "arbitrary"), vmem_limit_bytes=VMEM_LIMIT),
        name="mixers",
    )(proj, proj, proj, proj, proj, dt_raw, pos3, *consts)


def _merge_kernel(ys_ref, yr_ref, ga_ref, gb_ref, x_ref, gm_ref, shf_ref, scf_ref, nw_ref,
                  wbs_ref, wbr_ref, wo_ref, x1_ref, h2_ref):
    a = _dot(ys_ref[...], wbs_ref[...])
    b = _dot(yr_ref[...], wbr_ref[...])
    merged = ga_ref[...].astype(F32) * a + gb_ref[...].astype(F32) * b
    x1 = x_ref[...] + gm_ref[0] * _dot(merged.astype(BF16), wo_ref[...])
    x1_ref[...] = x1
    y = x1 * lax.rsqrt(jnp.mean(x1 * x1, axis=-1, keepdims=True) + NORM_EPS)
    h2_ref[...] = ((y * nw_ref[...]) * (1.0 + scf_ref[0]) + shf_ref[0]).astype(BF16)


def _merge_call(yssd, yret, proj, x2, mod3, norm_ffn_w, wbs, wbr, wo, seq, tm):
    t, d = x2.shape
    tps = seq // tm

    def tok(blk=0):
        return pl.BlockSpec((tm, d), lambda i: (i, blk))

    def modrow(k):
        return pl.BlockSpec((1, 1, d), lambda i: (i // tps, 0, k))

    return pl.pallas_call(
        _merge_kernel,
        grid=(t // tm,),
        in_specs=[tok(), tok(), tok(BLK_GA), tok(BLK_GB), tok(),
                  modrow(2), modrow(3), modrow(4), _resident((1, d)),
                  _resident(wbs.shape), _resident(wbr.shape), _resident(wo.shape)],
        out_specs=[tok(), tok()],
        out_shape=[jax.ShapeDtypeStruct((t, d), F32), jax.ShapeDtypeStruct((t, d), BF16)],
        compiler_params=pltpu.CompilerParams(
            dimension_semantics=("arbitrary",), vmem_limit_bytes=VMEM_LIMIT),
        name="merge",
    )(yssd, yret, proj, proj, x2, mod3, mod3, mod3, norm_ffn_w, wbs, wbr, wo)


def _ffn_kernel(h2_ref, x1_ref, gf_ref, nw_ref, wg_ref, wu_ref, wd_ref, o_ref):
    h2 = h2_ref[...]
    gate = _dot(h2, wg_ref[...])
    up = _dot(h2, wu_ref[...])
    down = _dot((_silu(gate) * up).astype(BF16), wd_ref[...])
    x2 = x1_ref[...] + gf_ref[0] * down
    y = x2 * lax.rsqrt(jnp.mean(x2 * x2, axis=-1, keepdims=True) + NORM_EPS)
    o_ref[...] = y * nw_ref[...]


def _ffn_call(h2, x1, mod3, norm_final_w, wg, wu, wd, seq, tm):
    t, d = x1.shape
    tps = seq // tm
    return pl.pallas_call(
        _ffn_kernel,
        grid=(t // tm,),
        in_specs=[pl.BlockSpec((tm, d), lambda i: (i, 0)),
                  pl.BlockSpec((tm, d), lambda i: (i, 0)),
                  pl.BlockSpec((1, 1, d), lambda i: (i // tps, 0, 5)),
                  _resident((1, d)),
                  _resident(wg.shape), _resident(wu.shape), _resident(wd.shape)],
        out_specs=pl.BlockSpec((tm, d), lambda i: (i, 0)),
        out_shape=jax.ShapeDtypeStruct((t, d), F32),
        compiler_params=pltpu.CompilerParams(
            dimension_semantics=("arbitrary",), vmem_limit_bytes=VMEM_LIMIT),
        name="ffn",
    )(h2, x1, mod3, norm_final_w, wg, wu, wd)


def _token_tile(seq):
    tm = min(TOKEN_TILE, seq)
    assert seq % tm == 0 and tm % CHUNK == 0
    return tm


def kernel(x, c, positions, w_ada, b_ada, norm_mix_w, w_in, ssd_conv_w, ssd_conv_b, ssd_dt_bias, ssd_A_log, ssd_D, ssd_norm_w, ret_norm_w, w_branch_ssd, w_branch_ret, w_out, norm_ffn_w, w_gate_up, w_down, norm_final_w):
    bsz, seq, d = x.shape
    assert d == D_MODEL and seq % CHUNK == 0
    assert w_ada.shape[0] == 1, "single-layer block"
    t = bsz * seq
    tm = _token_tile(seq)
    x2 = x.reshape(t, d)

    mod = _mod_call(c, w_ada[0], b_ada[0])
    mod3 = mod.reshape(bsz, 1, 6 * d)

    w_z, w_xbc, w_dt, w_q, w_k, w_v, w_g, w_ga, w_gb = jnp.split(w_in[0], np.cumsum(IN_SIZES)[:-1].tolist(), axis=1)
    w_main = jnp.concatenate([w_xbc, w_z, w_v, w_g, w_ga, w_gb, w_q, w_k], axis=1).astype(BF16)
    w_dtp = jnp.zeros((d, LANES), BF16).at[:, :SSD_HEADS].set(w_dt.astype(BF16))

    proj, dt_raw = _inproj_call(x2, mod3, norm_mix_w.reshape(1, d), w_main, w_dtp, seq, tm)

    pos3 = positions.astype(F32).reshape(t // CHUNK, 1, CHUNK)
    yssd, yret = _mixer_call(proj, dt_raw, pos3, bsz, seq, ssd_conv_w[0], ssd_conv_b[0], ssd_dt_bias[0],
                             ssd_A_log[0], ssd_D[0], ssd_norm_w[0], ret_norm_w[0])

    x1, h2 = _merge_call(yssd, yret, proj, x2, mod3, norm_ffn_w.reshape(1, d),
                         w_branch_ssd[0].astype(BF16), w_branch_ret[0].astype(BF16), w_out[0].astype(BF16),
                         seq, tm)

    wg = w_gate_up[0][:, :FFN_HIDDEN].astype(BF16)
    wu = w_gate_up[0][:, FFN_HIDDEN:].astype(BF16)
    out = _ffn_call(h2, x1, mod3, norm_final_w.reshape(1, d), wg, wu, w_down[0].astype(BF16), seq, tm)
    return out.reshape(bsz, seq, d)
```

```python
import functools
import math

import jax
import jax.numpy as jnp
import numpy as np
from jax import lax
from jax.experimental import pallas as pl
from jax.experimental.pallas import tpu as pltpu

F32 = jnp.float32
BF16 = jnp.bfloat16

D_MODEL = 1024
SSD_HEADS = 16
SSD_HEAD_DIM = 64
SSD_D_INNER = SSD_HEADS * SSD_HEAD_DIM
SSD_GROUPS = 4
SSD_HPG = SSD_HEADS // SSD_GROUPS
SSD_D_STATE = 128
SSD_CONV = 4
SSD_BC = SSD_GROUPS * SSD_D_STATE
SSD_CONV_DIM = SSD_D_INNER + 2 * SSD_BC
RET_HEADS = 4
RET_QK_DIM = 128
RET_V_DIM = 256
RET_D_QK = RET_HEADS * RET_QK_DIM
RET_D_V = RET_HEADS * RET_V_DIM
ROPE_BASE = 10000.0
CHUNK = 128
FFN_HIDDEN = 2816
NORM_EPS = 1e-6
IN_SIZES = (SSD_D_INNER, SSD_CONV_DIM, SSD_HEADS, RET_D_QK, RET_D_QK, RET_D_V, RET_D_V, D_MODEL, D_MODEL)

LANES = 128
PROJ_W = 6144
PROJ_BLK = 1024
BLK_Z, BLK_V, BLK_G, BLK_GA, BLK_GB, BLK_QK = 0, 1, 2, 3, 4, 5
BLK_XS, BLK_BC = 0, 1
HALO = 16
SILU_BLOCKS = (BLK_Z, BLK_G)
SIGMOID_BLOCKS = (BLK_GA, BLK_GB)
LOG2E = math.log2(math.e)
MASKED_LOG2 = -1e30

VMEM_LIMIT = 56 * 1024 * 1024
TOKEN_TILE = 512


def _silu(v):
    return v * (1.0 / (1.0 + jnp.exp(-v)))


def _sigmoid(v):
    return 1.0 / (1.0 + jnp.exp(-v))


def _dot(a, b):
    return jnp.dot(a, b, preferred_element_type=F32)


def _dot_nt(a, b):
    return lax.dot_general(a, b, (((1,), (1,)), ((), ())), preferred_element_type=F32)


def _dot_tn(a, b):
    return lax.dot_general(a, b, (((0,), (0,)), ((), ())), preferred_element_type=F32)


def _split3(v):
    hi = v.astype(BF16)
    r1 = v - hi.astype(F32)
    mid = r1.astype(BF16)
    lo = (r1 - mid.astype(F32)).astype(BF16)
    return hi, mid, lo


def _resident(shape):
    nd = len(shape)
    return pl.BlockSpec(shape, lambda *_: (0,) * nd, pipeline_mode=pl.Buffered(1))


def _mod_kernel(c_ref, w_ref, b_ref, o_ref):
    cond = _silu(c_ref[...])
    o_ref[...] = jnp.dot(cond, w_ref[...], preferred_element_type=F32,
                         precision=lax.Precision.HIGHEST) + b_ref[...]


def _mod_call(c, w_ada, b_ada):
    bsz, d = c.shape
    n = w_ada.shape[1]
    tn = 1024
    return pl.pallas_call(
        _mod_kernel,
        grid=(n // tn,),
        in_specs=[pl.BlockSpec((bsz, d), lambda j: (0, 0)),
                  pl.BlockSpec((d, tn), lambda j: (0, j)),
                  pl.BlockSpec((1, tn), lambda j: (0, j))],
        out_specs=pl.BlockSpec((bsz, tn), lambda j: (0, j)),
        out_shape=jax.ShapeDtypeStruct((bsz, n), F32),
        name="mod",
    )(c, w_ada, b_ada.reshape(1, n))


def _modulated_norm(x, nw_ref, scale_ref, shift_ref):
    y = x * lax.rsqrt(jnp.mean(x * x, axis=-1, keepdims=True) + NORM_EPS)
    return ((y * nw_ref[...]) * (1.0 + scale_ref[0]) + shift_ref[0]).astype(BF16)


def _conv_proj_kernel(tiles_per_seq, x_ref, halo_ref, shift_ref, scale_ref, nw_ref, w_ref, wdt_ref,
                      convw_ref, convb_ref, xbc_ref, dt_ref):
    tm = x_ref.shape[0]
    hb = _modulated_norm(x_ref[...], nw_ref, scale_ref, shift_ref)
    hb_halo = _modulated_norm(halo_ref[...], nw_ref, scale_ref, shift_ref)
    first_tile = (pl.program_id(0) % tiles_per_seq) == 0
    dt_ref[...] = _dot(hb, wdt_ref[...])
    for blk in range(SSD_CONV_DIM // PROJ_BLK):
        cols = slice(blk * PROJ_BLK, (blk + 1) * PROJ_BLK)
        r_halo = jnp.where(first_tile, 0.0, _dot(hb_halo, w_ref[:, cols]))
        xe = jnp.concatenate([r_halo, _dot(hb, w_ref[:, cols])], axis=0)
        acc = convw_ref[0:1, cols] * xe
        for j in range(1, SSD_CONV):
            acc = convw_ref[j:j + 1, cols] * xe + pltpu.roll(acc, 1, 0)
        xbc_ref[:, cols] = _silu(acc[HALO:, :] + convb_ref[:, cols]).astype(BF16)


def _gate_proj_kernel(x_ref, shift_ref, scale_ref, nw_ref, w_ref, proj_ref):
    hb = _modulated_norm(x_ref[...], nw_ref, scale_ref, shift_ref)
    for blk in range(PROJ_W // PROJ_BLK):
        cols = slice(blk * PROJ_BLK, (blk + 1) * PROJ_BLK)
        r = _dot(hb, w_ref[:, cols])
        if blk in SILU_BLOCKS:
            r = _silu(r)
        elif blk in SIGMOID_BLOCKS:
            r = _sigmoid(r)
        proj_ref[:, cols] = r.astype(BF16)


def _inproj_calls(x2, mod3, norm_w, w_xbc, w_dt, w_main, conv_w, conv_b, seq, tm):
    t, d = x2.shape
    tps = seq // tm
    halo_blocks_per_tile = tm // HALO
    params = pltpu.CompilerParams(dimension_semantics=("arbitrary",), vmem_limit_bytes=VMEM_LIMIT)
    tok = pl.BlockSpec((tm, d), lambda i: (i, 0))
    shift_m = pl.BlockSpec((1, 1, d), lambda i: (i // tps, 0, 0))
    scale_m = pl.BlockSpec((1, 1, d), lambda i: (i // tps, 0, 1))
    xbc, dt_raw = pl.pallas_call(
        functools.partial(_conv_proj_kernel, tps),
        grid=(t // tm,),
        in_specs=[tok,
                  pl.BlockSpec((HALO, d), lambda i: (jnp.maximum(i * halo_blocks_per_tile - 1, 0), 0)),
                  shift_m, scale_m, _resident((1, d)), _resident(w_xbc.shape), _resident(w_dt.shape),
                  _resident(conv_w.shape), _resident(conv_b.shape)],
        out_specs=[pl.BlockSpec((tm, SSD_CONV_DIM), lambda i: (i, 0)),
                   pl.BlockSpec((tm, LANES), lambda i: (i, 0))],
        out_shape=[jax.ShapeDtypeStruct((t, SSD_CONV_DIM), BF16),
                   jax.ShapeDtypeStruct((t, LANES), F32)],
        compiler_params=params,
        name="conv_proj",
    )(x2, x2, mod3, mod3, norm_w, w_xbc, w_dt, conv_w, conv_b)
    proj = pl.pallas_call(
        _gate_proj_kernel,
        grid=(t // tm,),
        in_specs=[tok, shift_m, scale_m, _resident((1, d)), _resident(w_main.shape)],
        out_specs=pl.BlockSpec((tm, PROJ_W), lambda i: (i, 0)),
        out_shape=jax.ShapeDtypeStruct((t, PROJ_W), BF16),
        compiler_params=params,
        name="gate_proj",
    )(x2, mod3, mod3, norm_w, w_main)
    return xbc, dt_raw, proj


def _mixer_kernel(xs_ref, bc_ref, zs_ref, v_ref, gs_ref, qk_ref, dt_ref, pos_ref,
                  dtb_ref, alog_ref, dexp_ref, snw_ref, rnw_ref,
                  tri_ref, hexp_ref, invf_ref, dmask_ref, qdec_ref, kdec_ref, cdec_ref,
                  yssd_ref, yret_ref,
                  xbd_ref, sstate_ref, rstate_ref):
    L = CHUNK
    P = SSD_HEAD_DIM
    GP = SSD_HPG * P

    @pl.when(pl.program_id(1) == 0)
    def _():
        xbd_ref[...] = jnp.zeros_like(xbd_ref)
        sstate_ref[...] = jnp.zeros_like(sstate_ref)
        rstate_ref[...] = jnp.zeros_like(rstate_ref)

    row = lax.broadcasted_iota(jnp.int32, (L, L), 0)
    col = lax.broadcasted_iota(jnp.int32, (L, L), 1)
    causal = row >= col

    dtr = dt_ref[...] + dtb_ref[...]
    dt = jnp.maximum(dtr, 0.0) + jnp.log1p(jnp.exp(-jnp.abs(dtr)))
    a2 = dt * (-LOG2E * jnp.exp(alog_ref[...]))
    a_hi, a_mid, a_lo = _split3(a2)
    acum3 = _dot(tri_ref[...], jnp.concatenate([a_hi, a_mid, a_lo], axis=1))
    acum = acum3[:, :LANES] + acum3[:, LANES:2 * LANES] + acum3[:, 2 * LANES:]
    a_last = acum[L - 1:L, :]
    src_t = (acum - jnp.log2(dt)).T
    per_head = jnp.concatenate([dt * jnp.exp2(a_last - acum), jnp.exp2(acum)], axis=0)
    hi = per_head.astype(BF16)
    mid = (per_head - hi.astype(F32)).astype(BF16)
    spread = _dot(jnp.concatenate([hi, mid], axis=1), hexp_ref[...])
    dtw_x, eacum_x = spread[:L], spread[L:]
    elast_x = eacum_x[L - 1:L, :]
    xs = xs_ref[...].astype(F32)
    xw = (xs * dtw_x).astype(BF16)
    dskip = dexp_ref[...] * xs

    for g in range(SSD_GROUPS):
        gcols = slice(g * GP, (g + 1) * GP)
        bg = bc_ref[:, g * SSD_D_STATE:(g + 1) * SSD_D_STATE]
        cg = bc_ref[:, SSD_BC + g * SSD_D_STATE:SSD_BC + (g + 1) * SSD_D_STATE]
        cb = _dot_nt(cg, bg)
        ms = []
        for r in range(SSD_HPG):
            h = g * SSD_HPG + r
            seg = acum[:, h:h + 1] - src_t[h:h + 1, :]
            ms.append((cb * jnp.exp2(jnp.where(causal, seg, MASKED_LOG2))).astype(BF16))
            xbd_ref[g, r * L:(r + 1) * L, r * P:(r + 1) * P] = xs_ref[:, g * GP + r * P:g * GP + (r + 1) * P]
        y = _dot(jnp.concatenate(ms, axis=1), xbd_ref[g])
        y = y + _dot(cg, sstate_ref[g].astype(BF16)) * eacum_x[:, gcols]
        sstate_ref[g] = sstate_ref[g] * elast_x[:, gcols] + _dot_tn(bg, xw[:, gcols])
        y = (y + dskip[:, gcols]) * zs_ref[:, gcols].astype(F32)
        y = y * lax.rsqrt(jnp.mean(y * y, axis=-1, keepdims=True) + NORM_EPS)
        yssd_ref[:, gcols] = (y * snw_ref[:, gcols]).astype(BF16)

    half = RET_QK_DIM // 2
    ang_t = invf_ref[...] * pos_ref[0]
    cos_t = jnp.cos(ang_t)
    sin_t = jnp.sin(ang_t)
    cos2 = jnp.concatenate([cos_t, cos_t], axis=0).T
    sin2 = jnp.concatenate([-sin_t, sin_t], axis=0).T
    kscale = RET_QK_DIM ** -0.5
    cos2k = cos2 * kscale
    sin2k = sin2 * kscale
    for h in range(RET_HEADS):
        qh = qk_ref[:, h * RET_QK_DIM:(h + 1) * RET_QK_DIM].astype(F32)
        kh = qk_ref[:, RET_D_QK + h * RET_QK_DIM:RET_D_QK + (h + 1) * RET_QK_DIM].astype(F32)
        qr = qh * cos2 + pltpu.roll(qh, half, 1) * sin2
        kr = kh * cos2k + pltpu.roll(kh, half, 1) * sin2k
        qb = qr.astype(BF16)
        vs = slice(h * RET_V_DIM, (h + 1) * RET_V_DIM)
        vh = v_ref[:, vs]
        scores = _dot_nt(qb, kr.astype(BF16)) * dmask_ref[h]
        y = _dot(scores.astype(BF16), vh)
        y = y + _dot(qb, rstate_ref[:, vs].astype(BF16)) * qdec_ref[:, vs]
        kd = (kr * kdec_ref[:, h * RET_QK_DIM:(h + 1) * RET_QK_DIM]).astype(BF16)
        rstate_ref[:, vs] = rstate_ref[:, vs] * cdec_ref[:, vs] + _dot_tn(kd, vh)
        y = y * lax.rsqrt(jnp.mean(y * y, axis=-1, keepdims=True) + NORM_EPS)
        yret_ref[:, vs] = (y * rnw_ref[:, vs] * gs_ref[:, vs].astype(F32)).astype(BF16)


def _mixer_consts():
    L = CHUNK
    idx = np.arange(L)
    tri = (idx[:, None] >= idx[None, :]).astype(np.float32)
    hexp = np.zeros((2 * LANES, SSD_D_INNER), np.float32)
    for piece in range(2):
        for h in range(SSD_HEADS):
            hexp[piece * LANES + h, h * SSD_HEAD_DIM:(h + 1) * SSD_HEAD_DIM] = 1.0
    return jnp.asarray(tri, BF16), jnp.asarray(hexp, BF16)


def _retention_tables():
    L = CHUNK
    H = RET_HEADS
    log_gamma = jnp.log1p(-jnp.exp2(-5.0 - jnp.arange(H, dtype=F32)))
    idx = jnp.arange(L, dtype=F32)
    diff = idx[:, None] - idx[None, :]
    dmask = jnp.where(diff >= 0, jnp.exp(jnp.maximum(diff, 0.0)[None] * log_gamma[:, None, None]), 0.0)
    q_decay = jnp.exp((idx + 1.0)[:, None] * log_gamma)
    k_decay = jnp.exp((L - 1.0 - idx)[:, None] * log_gamma)
    chunk_decay = jnp.exp(L * log_gamma)
    qdec = jnp.repeat(q_decay, RET_V_DIM, axis=1)
    kdec = jnp.repeat(k_decay, RET_QK_DIM, axis=1)
    cdec = jnp.repeat(chunk_decay, RET_V_DIM)[None, :]
    half = RET_QK_DIM // 2
    inv_freq = jnp.power(ROPE_BASE, -jnp.arange(half, dtype=F32) / half)
    invf = jnp.broadcast_to(inv_freq[:, None], (half, LANES))
    return dmask, qdec, kdec, cdec, invf


def _mixer_call(xbc, proj, dt_raw, pos3, bsz, seq, dt_bias, a_log, d_skip, ssd_norm_w, ret_norm_w):
    t = proj.shape[0]
    nc = seq // CHUNK
    tri, hexp = _mixer_consts()
    dmask, qdec, kdec, cdec, invf = _retention_tables()

    def pad_heads(v):
        return jnp.zeros((1, LANES), F32).at[0, :SSD_HEADS].set(v.astype(F32))

    def tok(width, blk):
        return pl.BlockSpec((CHUNK, width), lambda b, c: (b * nc + c, blk * PROJ_BLK // width))

    consts = [pad_heads(dt_bias), pad_heads(a_log),
              jnp.repeat(d_skip.astype(F32), SSD_HEAD_DIM)[None, :], ssd_norm_w.reshape(1, -1).astype(F32),
              ret_norm_w.reshape(1, -1).astype(F32), tri, hexp, invf, dmask, qdec, kdec, cdec]
    in_specs = [tok(SSD_D_INNER, BLK_XS), tok(2 * SSD_BC, BLK_BC), tok(SSD_D_INNER, BLK_Z), tok(RET_D_V, BLK_V),
                tok(RET_D_V, BLK_G), tok(2 * RET_D_QK, BLK_QK), tok(LANES, 0),
                pl.BlockSpec((1, 1, CHUNK), lambda b, c: (b * nc + c, 0, 0))]
    in_specs += [_resident(v.shape) for v in consts]
    return pl.pallas_call(
        _mixer_kernel,
        grid=(bsz, nc),
        in_specs=in_specs,
        out_specs=[pl.BlockSpec((CHUNK, SSD_D_INNER), lambda b, c: (b * nc + c, 0)),
                   pl.BlockSpec((CHUNK, RET_D_V), lambda b, c: (b * nc + c, 0))],
        out_shape=[jax.ShapeDtypeStruct((t, SSD_D_INNER), BF16),
                   jax.ShapeDtypeStruct((t, RET_D_V), BF16)],
        scratch_shapes=[pltpu.VMEM((SSD_GROUPS, SSD_HPG * CHUNK, SSD_HPG * SSD_HEAD_DIM), BF16),
                        pltpu.VMEM((SSD_GROUPS, SSD_D_STATE, SSD_HPG * SSD_HEAD_DIM), F32),
                        pltpu.VMEM((RET_QK_DIM, RET_D_V), F32)],
        compiler_params=pltpu.CompilerParams(
            dimension_semantics=("arbitrary", "arbitrary"), vmem_limit_bytes=VMEM_LIMIT),
        name="mixers",
    )(xbc, xbc, proj, proj, proj, proj, dt_raw, pos3, *consts)


def _merge_kernel(ys_ref, yr_ref, ga_ref, gb_ref, x_ref, gm_ref, shf_ref, scf_ref, nw_ref,
                  wbs_ref, wbr_ref, wo_ref, x1_ref, h2_ref):
    a = _dot(ys_ref[...], wbs_ref[...])
    b = _dot(yr_ref[...], wbr_ref[...])
    merged = ga_ref[...].astype(F32) * a + gb_ref[...].astype(F32) * b
    x1 = x_ref[...] + gm_ref[0] * _dot(merged.astype(BF16), wo_ref[...])
    x1_ref[...] = x1
    y = x1 * lax.rsqrt(jnp.mean(x1 * x1, axis=-1, keepdims=True) + NORM_EPS)
    h2_ref[...] = ((y * nw_ref[...]) * (1.0 + scf_ref[0]) + shf_ref[0]).astype(BF16)


def _merge_call(yssd, yret, proj, x2, mod3, norm_ffn_w, wbs, wbr, wo, seq, tm):
    t, d = x2.shape
    tps = seq // tm

    def tok(blk=0):
        return pl.BlockSpec((tm, d), lambda i: (i, blk))

    def modrow(k):
        return pl.BlockSpec((1, 1, d), lambda i: (i // tps, 0, k))

    return pl.pallas_call(
        _merge_kernel,
        grid=(t // tm,),
        in_specs=[tok(), tok(), tok(BLK_GA), tok(BLK_GB), tok(),
                  modrow(2), modrow(3), modrow(4), _resident((1, d)),
                  _resident(wbs.shape), _resident(wbr.shape), _resident(wo.shape)],
        out_specs=[tok(), tok()],
        out_shape=[jax.ShapeDtypeStruct((t, d), F32), jax.ShapeDtypeStruct((t, d), BF16)],
        compiler_params=pltpu.CompilerParams(
            dimension_semantics=("arbitrary",), vmem_limit_bytes=VMEM_LIMIT),
        name="merge",
    )(yssd, yret, proj, proj, x2, mod3, mod3, mod3, norm_ffn_w, wbs, wbr, wo)


def _ffn_kernel(h2_ref, x1_ref, gf_ref, nw_ref, wg_ref, wu_ref, wd_ref, o_ref):
    h2 = h2_ref[...]
    gate = _dot(h2, wg_ref[...])
    up = _dot(h2, wu_ref[...])
    down = _dot((_silu(gate) * up).astype(BF16), wd_ref[...])
    x2 = x1_ref[...] + gf_ref[0] * down
    y = x2 * lax.rsqrt(jnp.mean(x2 * x2, axis=-1, keepdims=True) + NORM_EPS)
    o_ref[...] = y * nw_ref[...]


def _ffn_call(h2, x1, mod3, norm_final_w, wg, wu, wd, seq, tm):
    t, d = x1.shape
    tps = seq // tm
    return pl.pallas_call(
        _ffn_kernel,
        grid=(t // tm,),
        in_specs=[pl.BlockSpec((tm, d), lambda i: (i, 0)),
                  pl.BlockSpec((tm, d), lambda i: (i, 0)),
                  pl.BlockSpec((1, 1, d), lambda i: (i // tps, 0, 5)),
                  _resident((1, d)),
                  _resident(wg.shape), _resident(wu.shape), _resident(wd.shape)],
        out_specs=pl.BlockSpec((tm, d), lambda i: (i, 0)),
        out_shape=jax.ShapeDtypeStruct((t, d), F32),
        compiler_params=pltpu.CompilerParams(
            dimension_semantics=("arbitrary",), vmem_limit_bytes=VMEM_LIMIT),
        name="ffn",
    )(h2, x1, mod3, norm_final_w, wg, wu, wd)


def _token_tile(seq):
    tm = min(TOKEN_TILE, seq)
    assert seq % tm == 0 and tm % CHUNK == 0
    return tm


def kernel(x, c, positions, w_ada, b_ada, norm_mix_w, w_in, ssd_conv_w, ssd_conv_b, ssd_dt_bias, ssd_A_log, ssd_D, ssd_norm_w, ret_norm_w, w_branch_ssd, w_branch_ret, w_out, norm_ffn_w, w_gate_up, w_down, norm_final_w):
    bsz, seq, d = x.shape
    assert d == D_MODEL and seq % CHUNK == 0
    assert w_ada.shape[0] == 1, "single-layer block"
    t = bsz * seq
    tm = _token_tile(seq)
    x2 = x.reshape(t, d)

    mod = _mod_call(c, w_ada[0], b_ada[0])
    mod3 = mod.reshape(bsz, 1, 6 * d)

    w_z, w_xbc, w_dt, w_q, w_k, w_v, w_g, w_ga, w_gb = jnp.split(w_in[0], np.cumsum(IN_SIZES)[:-1].tolist(), axis=1)
    w_main = jnp.concatenate([w_z, w_v, w_g, w_ga, w_gb, w_q, w_k], axis=1).astype(BF16)
    w_dtp = jnp.zeros((d, LANES), BF16).at[:, :SSD_HEADS].set(w_dt.astype(BF16))

    xbc, dt_raw, proj = _inproj_calls(x2, mod3, norm_mix_w.reshape(1, d), w_xbc.astype(BF16), w_dtp, w_main,
                                      ssd_conv_w[0].astype(F32), ssd_conv_b[0].reshape(1, -1).astype(F32), seq, tm)

    pos3 = positions.astype(F32).reshape(t // CHUNK, 1, CHUNK)
    yssd, yret = _mixer_call(xbc, proj, dt_raw, pos3, bsz, seq, ssd_dt_bias[0], ssd_A_log[0], ssd_D[0],
                             ssd_norm_w[0], ret_norm_w[0])

    x1, h2 = _merge_call(yssd, yret, proj, x2, mod3, norm_ffn_w.reshape(1, d),
                         w_branch_ssd[0].astype(BF16), w_branch_ret[0].astype(BF16), w_out[0].astype(BF16),
                         seq, tm)

    wg = w_gate_up[0][:, :FFN_HIDDEN].astype(BF16)
    wu = w_gate_up[0][:, FFN_HIDDEN:].astype(BF16)
    out = _ffn_call(h2, x1, mod3, norm_final_w.reshape(1, d), wg, wu, w_down[0].astype(BF16), seq, tm)
    return out.reshape(bsz, seq, d)
```

```python
import functools
import math

import jax
import jax.numpy as jnp
import numpy as np
from jax import lax
from jax.experimental import pallas as pl
from jax.experimental.pallas import tpu as pltpu

F32 = jnp.float32
BF16 = jnp.bfloat16

D_MODEL = 1024
SSD_HEADS = 16
SSD_HEAD_DIM = 64
SSD_D_INNER = SSD_HEADS * SSD_HEAD_DIM
SSD_GROUPS = 4
SSD_HPG = SSD_HEADS // SSD_GROUPS
SSD_D_STATE = 128
SSD_CONV = 4
SSD_BC = SSD_GROUPS * SSD_D_STATE
SSD_CONV_DIM = SSD_D_INNER + 2 * SSD_BC
RET_HEADS = 4
RET_QK_DIM = 128
RET_V_DIM = 256
RET_D_QK = RET_HEADS * RET_QK_DIM
RET_D_V = RET_HEADS * RET_V_DIM
ROPE_BASE = 10000.0
CHUNK = 128
FFN_HIDDEN = 2816
NORM_EPS = 1e-6
IN_SIZES = (SSD_D_INNER, SSD_CONV_DIM, SSD_HEADS, RET_D_QK, RET_D_QK, RET_D_V, RET_D_V, D_MODEL, D_MODEL)

LANES = 128
PROJ_W = 6144
PROJ_BLK = 1024
BLK_Z, BLK_V, BLK_G, BLK_GA, BLK_GB, BLK_QK = 0, 1, 2, 3, 4, 5
BLK_XS, BLK_BC = 0, 1
HALO = 16
SILU_BLOCKS = (BLK_Z, BLK_G)
SIGMOID_BLOCKS = (BLK_GA, BLK_GB)
LOG2E = math.log2(math.e)
MASKED_LOG2 = -1e30

VMEM_LIMIT = 56 * 1024 * 1024
TOKEN_TILE = 512


def _silu(v):
    return v * (1.0 / (1.0 + jnp.exp(-v)))


def _sigmoid(v):
    return 1.0 / (1.0 + jnp.exp(-v))


def _dot(a, b):
    return jnp.dot(a, b, preferred_element_type=F32)


def _dot_nt(a, b):
    return lax.dot_general(a, b, (((1,), (1,)), ((), ())), preferred_element_type=F32)


def _dot_tn(a, b):
    return lax.dot_general(a, b, (((0,), (0,)), ((), ())), preferred_element_type=F32)


def _split3(v):
    hi = v.astype(BF16)
    r1 = v - hi.astype(F32)
    mid = r1.astype(BF16)
    lo = (r1 - mid.astype(F32)).astype(BF16)
    return hi, mid, lo


def _resident(shape):
    nd = len(shape)
    return pl.BlockSpec(shape, lambda *_: (0,) * nd, pipeline_mode=pl.Buffered(1))


def _mod_kernel(c_ref, w_ref, b_ref, o_ref):
    cond = _silu(c_ref[...])
    o_ref[...] = jnp.dot(cond, w_ref[...], preferred_element_type=F32,
                         precision=lax.Precision.HIGHEST) + b_ref[...]


def _mod_call(c, w_ada, b_ada):
    bsz, d = c.shape
    n = w_ada.shape[1]
    tn = 1024
    return pl.pallas_call(
        _mod_kernel,
        grid=(n // tn,),
        in_specs=[pl.BlockSpec((bsz, d), lambda j: (0, 0)),
                  pl.BlockSpec((d, tn), lambda j: (0, j)),
                  pl.BlockSpec((1, tn), lambda j: (0, j))],
        out_specs=pl.BlockSpec((bsz, tn), lambda j: (0, j)),
        out_shape=jax.ShapeDtypeStruct((bsz, n), F32),
        name="mod",
    )(c, w_ada, b_ada.reshape(1, n))


def _modulated_norm(x, nw_ref, scale_ref, shift_ref):
    y = x * lax.rsqrt(jnp.mean(x * x, axis=-1, keepdims=True) + NORM_EPS)
    return ((y * nw_ref[...]) * (1.0 + scale_ref[0]) + shift_ref[0]).astype(BF16)


def _conv_proj_kernel(tiles_per_seq, x_ref, halo_ref, shift_ref, scale_ref, nw_ref, w_ref, wdt_ref,
                      convw_ref, convb_ref, xbc_ref, dt_ref, h_ref):
    hb = _modulated_norm(x_ref[...], nw_ref, scale_ref, shift_ref)
    h_ref[...] = hb
    hb_halo = _modulated_norm(halo_ref[...], nw_ref, scale_ref, shift_ref)
    first_tile = (pl.program_id(0) % tiles_per_seq) == 0
    dt_ref[...] = _dot(hb, wdt_ref[...])
    for blk in range(SSD_CONV_DIM // PROJ_BLK):
        cols = slice(blk * PROJ_BLK, (blk + 1) * PROJ_BLK)
        r_halo = jnp.where(first_tile, 0.0, _dot(hb_halo, w_ref[:, cols]))
        xe = jnp.concatenate([r_halo, _dot(hb, w_ref[:, cols])], axis=0)
        acc = convw_ref[0:1, cols] * xe
        for j in range(1, SSD_CONV):
            acc = convw_ref[j:j + 1, cols] * xe + pltpu.roll(acc, 1, 0)
        xbc_ref[:, cols] = _silu(acc[HALO:, :] + convb_ref[:, cols]).astype(BF16)


def _gate_proj_kernel(h_ref, pos_ref, invf_ref, w_ref, proj_ref):
    hb = h_ref[...]
    half = RET_QK_DIM // 2
    ang_t = invf_ref[...] * pos_ref[0]
    cos_t = jnp.cos(ang_t)
    sin_t = jnp.sin(ang_t)
    cos2 = jnp.concatenate([cos_t, cos_t], axis=0).T
    sin2 = jnp.concatenate([-sin_t, sin_t], axis=0).T
    kscale = RET_QK_DIM ** -0.5
    for blk in range(PROJ_W // PROJ_BLK):
        cols = slice(blk * PROJ_BLK, (blk + 1) * PROJ_BLK)
        r = _dot(hb, w_ref[:, cols])
        if blk == BLK_QK:
            for hd in range(PROJ_BLK // RET_QK_DIM):
                hcols = slice(hd * RET_QK_DIM, (hd + 1) * RET_QK_DIM)
                t_h = r[:, hcols]
                rot = t_h * cos2 + pltpu.roll(t_h, half, 1) * sin2
                if hd >= RET_HEADS:
                    rot = rot * kscale
                proj_ref[:, blk * PROJ_BLK + hd * RET_QK_DIM:blk * PROJ_BLK + (hd + 1) * RET_QK_DIM] = rot.astype(BF16)
            continue
        if blk in SILU_BLOCKS:
            r = _silu(r)
        elif blk in SIGMOID_BLOCKS:
            r = _sigmoid(r)
        proj_ref[:, cols] = r.astype(BF16)


def _inproj_calls(x2, positions, mod3, norm_w, w_xbc, w_dt, w_main, conv_w, conv_b, seq, tm):
    t, d = x2.shape
    tps = seq // tm
    halo_blocks_per_tile = tm // HALO
    params = pltpu.CompilerParams(dimension_semantics=("arbitrary",), vmem_limit_bytes=VMEM_LIMIT)
    tok = pl.BlockSpec((tm, d), lambda i: (i, 0))
    shift_m = pl.BlockSpec((1, 1, d), lambda i: (i // tps, 0, 0))
    scale_m = pl.BlockSpec((1, 1, d), lambda i: (i // tps, 0, 1))
    xbc, dt_raw, h = pl.pallas_call(
        functools.partial(_conv_proj_kernel, tps),
        grid=(t // tm,),
        in_specs=[tok,
                  pl.BlockSpec((HALO, d), lambda i: (jnp.maximum(i * halo_blocks_per_tile - 1, 0), 0)),
                  shift_m, scale_m, _resident((1, d)), _resident(w_xbc.shape), _resident(w_dt.shape),
                  _resident(conv_w.shape), _resident(conv_b.shape)],
        out_specs=[pl.BlockSpec((tm, SSD_CONV_DIM), lambda i: (i, 0)),
                   pl.BlockSpec((tm, LANES), lambda i: (i, 0)),
                   tok],
        out_shape=[jax.ShapeDtypeStruct((t, SSD_CONV_DIM), BF16),
                   jax.ShapeDtypeStruct((t, LANES), F32),
                   jax.ShapeDtypeStruct((t, d), BF16)],
        compiler_params=params,
        name="conv_proj",
    )(x2, x2, mod3, mod3, norm_w, w_xbc, w_dt, conv_w, conv_b)
    half = RET_QK_DIM // 2
    inv_freq = jnp.power(ROPE_BASE, -jnp.arange(half, dtype=F32) / half)
    invf = jnp.broadcast_to(inv_freq[:, None], (half, tm))
    proj = pl.pallas_call(
        _gate_proj_kernel,
        grid=(t // tm,),
        in_specs=[tok, pl.BlockSpec((1, 1, tm), lambda i: (i, 0, 0)), _resident(invf.shape),
                  _resident(w_main.shape)],
        out_specs=pl.BlockSpec((tm, PROJ_W), lambda i: (i, 0)),
        out_shape=jax.ShapeDtypeStruct((t, PROJ_W), BF16),
        compiler_params=params,
        name="gate_proj",
    )(h, positions.astype(F32).reshape(t // tm, 1, tm), invf, w_main)
    return xbc, dt_raw, proj


def _decay_terms(dt_raw, dtb_ref, alog_ref, tri_ref, hexp_ref, head_ref, wide_ref):
    L = CHUNK
    dtr = dt_raw + dtb_ref[...]
    dt = jnp.maximum(dtr, 0.0) + jnp.log1p(jnp.exp(-jnp.abs(dtr)))
    a2 = dt * (-LOG2E * jnp.exp(alog_ref[...]))
    a_hi, a_mid, a_lo = _split3(a2)
    acum3 = _dot(tri_ref[...], jnp.concatenate([a_hi, a_mid, a_lo], axis=1))
    acum = acum3[:, :LANES] + acum3[:, LANES:2 * LANES] + acum3[:, 2 * LANES:]
    a_last = acum[L - 1:L, :]
    head_ref[0] = acum
    head_ref[1] = (acum - jnp.log2(dt)).T
    per_head = jnp.concatenate([dt * jnp.exp2(a_last - acum), jnp.exp2(acum)], axis=0)
    hi = per_head.astype(BF16)
    mid = (per_head - hi.astype(F32)).astype(BF16)
    spread = _dot(jnp.concatenate([hi, mid], axis=1), hexp_ref[...])
    wide_ref[0] = spread[:L]
    wide_ref[1] = spread[L:]


def _mixer_kernel(xs_ref, bc_ref, zs_ref, v_ref, gs_ref, qk_ref, dt_ref, dtn_ref,
                  dtb_ref, alog_ref, dexp_ref, snw_ref, rnw_ref,
                  tri_ref, hexp_ref, dmask_ref, qdec_ref, kdec_ref, cdec_ref,
                  yssd_ref, yret_ref,
                  xbd_ref, sstate_ref, rstate_ref, dhead_ref, dwide_ref):
    L = CHUNK
    P = SSD_HEAD_DIM
    GP = SSD_HPG * P
    step = pl.program_id(0) * pl.num_programs(1) + pl.program_id(1)
    slot = step % 2

    @pl.when(pl.program_id(1) == 0)
    def _():
        xbd_ref[...] = jnp.zeros_like(xbd_ref)
        sstate_ref[...] = jnp.zeros_like(sstate_ref)
        rstate_ref[...] = jnp.zeros_like(rstate_ref)

    @pl.when(step == 0)
    def _():
        _decay_terms(dt_ref[...], dtb_ref, alog_ref, tri_ref, hexp_ref, dhead_ref.at[0], dwide_ref.at[0])

    row = lax.broadcasted_iota(jnp.int32, (L, L), 0)
    col = lax.broadcasted_iota(jnp.int32, (L, L), 1)
    causal = row >= col

    acum = dhead_ref[slot, 0]
    src_t = dhead_ref[slot, 1]
    dtw_x = dwide_ref.at[slot, 0]
    eacum_x = dwide_ref.at[slot, 1]
    xs = xs_ref[...].astype(F32)
    xw = (xs * dtw_x[...]).astype(BF16)
    dskip = dexp_ref[...] * xs

    for g in range(SSD_GROUPS):
        gcols = slice(g * GP, (g + 1) * GP)
        bg = bc_ref[:, g * SSD_D_STATE:(g + 1) * SSD_D_STATE]
        cg = bc_ref[:, SSD_BC + g * SSD_D_STATE:SSD_BC + (g + 1) * SSD_D_STATE]
        cb = _dot_nt(cg, bg)
        ms = []
        for r in range(SSD_HPG):
            h = g * SSD_HPG + r
            seg = acum[:, h:h + 1] - src_t[h:h + 1, :]
            ms.append((cb * jnp.exp2(jnp.where(causal, seg, MASKED_LOG2))).astype(BF16))
            xbd_ref[g, r * L:(r + 1) * L, r * P:(r + 1) * P] = xs_ref[:, g * GP + r * P:g * GP + (r + 1) * P]
        y = _dot(jnp.concatenate(ms, axis=1), xbd_ref[g])
        y = y + _dot(cg, sstate_ref[g].astype(BF16)) * eacum_x[:, gcols]
        sstate_ref[g] = sstate_ref[g] * eacum_x[L - 1:L, gcols] + _dot_tn(bg, xw[:, gcols])
        y = (y + dskip[:, gcols]) * zs_ref[:, gcols].astype(F32)
        y = y * lax.rsqrt(jnp.mean(y * y, axis=-1, keepdims=True) + NORM_EPS)
        yssd_ref[:, gcols] = (y * snw_ref[:, gcols]).astype(BF16)

    for h in range(RET_HEADS):
        qb = qk_ref[:, h * RET_QK_DIM:(h + 1) * RET_QK_DIM]
        kb = qk_ref[:, RET_D_QK + h * RET_QK_DIM:RET_D_QK + (h + 1) * RET_QK_DIM]
        vs = slice(h * RET_V_DIM, (h + 1) * RET_V_DIM)
        vh = v_ref[:, vs]
        scores = _dot_nt(qb, kb) * dmask_ref[h]
        y = _dot(scores.astype(BF16), vh)
        y = y + _dot(qb, rstate_ref[:, vs].astype(BF16)) * qdec_ref[:, vs]
        kd = (kb.astype(F32) * kdec_ref[:, h * RET_QK_DIM:(h + 1) * RET_QK_DIM]).astype(BF16)
        rstate_ref[:, vs] = rstate_ref[:, vs] * cdec_ref[:, vs] + _dot_tn(kd, vh)
        y = y * lax.rsqrt(jnp.mean(y * y, axis=-1, keepdims=True) + NORM_EPS)
        yret_ref[:, vs] = (y * rnw_ref[:, vs] * gs_ref[:, vs].astype(F32)).astype(BF16)

    _decay_terms(dtn_ref[...], dtb_ref, alog_ref, tri_ref, hexp_ref, dhead_ref.at[1 - slot], dwide_ref.at[1 - slot])


def _mixer_consts():
    L = CHUNK
    idx = np.arange(L)
    tri = (idx[:, None] >= idx[None, :]).astype(np.float32)
    hexp = np.zeros((2 * LANES, SSD_D_INNER), np.float32)
    for piece in range(2):
        for h in range(SSD_HEADS):
            hexp[piece * LANES + h, h * SSD_HEAD_DIM:(h + 1) * SSD_HEAD_DIM] = 1.0
    return jnp.asarray(tri, BF16), jnp.asarray(hexp, BF16)


def _retention_tables():
    L = CHUNK
    H = RET_HEADS
    log_gamma = jnp.log1p(-jnp.exp2(-5.0 - jnp.arange(H, dtype=F32)))
    idx = jnp.arange(L, dtype=F32)
    diff = idx[:, None] - idx[None, :]
    dmask = jnp.where(diff >= 0, jnp.exp(jnp.maximum(diff, 0.0)[None] * log_gamma[:, None, None]), 0.0)
    q_decay = jnp.exp((idx + 1.0)[:, None] * log_gamma)
    k_decay = jnp.exp((L - 1.0 - idx)[:, None] * log_gamma)
    chunk_decay = jnp.exp(L * log_gamma)
    qdec = jnp.repeat(q_decay, RET_V_DIM, axis=1)
    kdec = jnp.repeat(k_decay, RET_QK_DIM, axis=1)
    cdec = jnp.repeat(chunk_decay, RET_V_DIM)[None, :]
    return dmask, qdec, kdec, cdec


def _mixer_call(xbc, proj, dt_raw, bsz, seq, dt_bias, a_log, d_skip, ssd_norm_w, ret_norm_w):
    t = proj.shape[0]
    nc = seq // CHUNK
    tri, hexp = _mixer_consts()
    dmask, qdec, kdec, cdec = _retention_tables()

    def pad_heads(v):
        return jnp.zeros((1, LANES), F32).at[0, :SSD_HEADS].set(v.astype(F32))

    def tok(width, blk):
        return pl.BlockSpec((CHUNK, width), lambda b, c: (b * nc + c, blk * PROJ_BLK // width))

    consts = [pad_heads(dt_bias), pad_heads(a_log),
              jnp.repeat(d_skip.astype(F32), SSD_HEAD_DIM)[None, :], ssd_norm_w.reshape(1, -1).astype(F32),
              ret_norm_w.reshape(1, -1).astype(F32), tri, hexp, dmask, qdec, kdec, cdec]
    in_specs = [tok(SSD_D_INNER, BLK_XS), tok(2 * SSD_BC, BLK_BC), tok(SSD_D_INNER, BLK_Z), tok(RET_D_V, BLK_V),
                tok(RET_D_V, BLK_G), tok(2 * RET_D_QK, BLK_QK), tok(LANES, 0),
                pl.BlockSpec((CHUNK, LANES), lambda b, c: (jnp.minimum(b * nc + c + 1, bsz * nc - 1), 0))]
    in_specs += [_resident(v.shape) for v in consts]
    return pl.pallas_call(
        _mixer_kernel,
        grid=(bsz, nc),
        in_specs=in_specs,
        out_specs=[pl.BlockSpec((CHUNK, SSD_D_INNER), lambda b, c: (b * nc + c, 0)),
                   pl.BlockSpec((CHUNK, RET_D_V), lambda b, c: (b * nc + c, 0))],
        out_shape=[jax.ShapeDtypeStruct((t, SSD_D_INNER), BF16),
                   jax.ShapeDtypeStruct((t, RET_D_V), BF16)],
        scratch_shapes=[pltpu.VMEM((SSD_GROUPS, SSD_HPG * CHUNK, SSD_HPG * SSD_HEAD_DIM), BF16),
                        pltpu.VMEM((SSD_GROUPS, SSD_D_STATE, SSD_HPG * SSD_HEAD_DIM), F32),
                        pltpu.VMEM((RET_QK_DIM, RET_D_V), F32),
                        pltpu.VMEM((2, 2, CHUNK, LANES), F32),
                        pltpu.VMEM((2, 2, CHUNK, SSD_D_INNER), F32)],
        compiler_params=pltpu.CompilerParams(
            dimension_semantics=("arbitrary", "arbitrary"), vmem_limit_bytes=VMEM_LIMIT),
        name="mixers",
    )(xbc, xbc, proj, proj, proj, proj, dt_raw, dt_raw, *consts)


def _merge_kernel(ys_ref, yr_ref, ga_ref, gb_ref, x_ref, gm_ref, shf_ref, scf_ref, nw_ref,
                  wbs_ref, wbr_ref, wo_ref, x1_ref, h2_ref):
    a = _dot(ys_ref[...], wbs_ref[...])
    b = _dot(yr_ref[...], wbr_ref[...])
    merged = ga_ref[...].astype(F32) * a + gb_ref[...].astype(F32) * b
    x1 = x_ref[...] + gm_ref[0] * _dot(merged.astype(BF16), wo_ref[...])
    x1_ref[...] = x1
    y = x1 * lax.rsqrt(jnp.mean(x1 * x1, axis=-1, keepdims=True) + NORM_EPS)
    h2_ref[...] = ((y * nw_ref[...]) * (1.0 + scf_ref[0]) + shf_ref[0]).astype(BF16)


def _merge_call(yssd, yret, proj, x2, mod3, norm_ffn_w, wbs, wbr, wo, seq, tm):
    t, d = x2.shape
    tps = seq // tm

    def tok(blk=0):
        return pl.BlockSpec((tm, d), lambda i: (i, blk))

    def modrow(k):
        return pl.BlockSpec((1, 1, d), lambda i: (i // tps, 0, k))

    return pl.pallas_call(
        _merge_kernel,
        grid=(t // tm,),
        in_specs=[tok(), tok(), tok(BLK_GA), tok(BLK_GB), tok(),
                  modrow(2), modrow(3), modrow(4), _resident((1, d)),
                  _resident(wbs.shape), _resident(wbr.shape), _resident(wo.shape)],
        out_specs=[tok(), tok()],
        out_shape=[jax.ShapeDtypeStruct((t, d), F32), jax.ShapeDtypeStruct((t, d), BF16)],
        compiler_params=pltpu.CompilerParams(
            dimension_semantics=("arbitrary",), vmem_limit_bytes=VMEM_LIMIT),
        name="merge",
    )(yssd, yret, proj, proj, x2, mod3, mod3, mod3, norm_ffn_w, wbs, wbr, wo)


def _ffn_kernel(h2_ref, x1_ref, gf_ref, nw_ref, wg_ref, wu_ref, wd_ref, o_ref):
    h2 = h2_ref[...]
    gate = _dot(h2, wg_ref[...])
    up = _dot(h2, wu_ref[...])
    down = _dot((_silu(gate) * up).astype(BF16), wd_ref[...])
    x2 = x1_ref[...] + gf_ref[0] * down
    y = x2 * lax.rsqrt(jnp.mean(x2 * x2, axis=-1, keepdims=True) + NORM_EPS)
    o_ref[...] = y * nw_ref[...]


def _ffn_call(h2, x1, mod3, norm_final_w, wg, wu, wd, seq, tm):
    t, d = x1.shape
    tps = seq // tm
    return pl.pallas_call(
        _ffn_kernel,
        grid=(t // tm,),
        in_specs=[pl.BlockSpec((tm, d), lambda i: (i, 0)),
                  pl.BlockSpec((tm, d), lambda i: (i, 0)),
                  pl.BlockSpec((1, 1, d), lambda i: (i // tps, 0, 5)),
                  _resident((1, d)),
                  _resident(wg.shape), _resident(wu.shape), _resident(wd.shape)],
        out_specs=pl.BlockSpec((tm, d), lambda i: (i, 0)),
        out_shape=jax.ShapeDtypeStruct((t, d), F32),
        compiler_params=pltpu.CompilerParams(
            dimension_semantics=("arbitrary",), vmem_limit_bytes=VMEM_LIMIT),
        name="ffn",
    )(h2, x1, mod3, norm_final_w, wg, wu, wd)


def _token_tile(seq):
    tm = min(TOKEN_TILE, seq)
    assert seq % tm == 0 and tm % CHUNK == 0
    return tm


def kernel(x, c, positions, w_ada, b_ada, norm_mix_w, w_in, ssd_conv_w, ssd_conv_b, ssd_dt_bias, ssd_A_log, ssd_D, ssd_norm_w, ret_norm_w, w_branch_ssd, w_branch_ret, w_out, norm_ffn_w, w_gate_up, w_down, norm_final_w):
    bsz, seq, d = x.shape
    assert d == D_MODEL and seq % CHUNK == 0
    assert w_ada.shape[0] == 1, "single-layer block"
    t = bsz * seq
    tm = _token_tile(seq)
    x2 = x.reshape(t, d)

    mod = _mod_call(c, w_ada[0], b_ada[0])
    mod3 = mod.reshape(bsz, 1, 6 * d)

    w_z, w_xbc, w_dt, w_q, w_k, w_v, w_g, w_ga, w_gb = jnp.split(w_in[0], np.cumsum(IN_SIZES)[:-1].tolist(), axis=1)
    w_main = jnp.concatenate([w_z, w_v, w_g, w_ga, w_gb, w_q, w_k], axis=1).astype(BF16)
    w_dtp = jnp.zeros((d, LANES), BF16).at[:, :SSD_HEADS].set(w_dt.astype(BF16))

    xbc, dt_raw, proj = _inproj_calls(x2, positions, mod3, norm_mix_w.reshape(1, d), w_xbc.astype(BF16), w_dtp,
                                      w_main, ssd_conv_w[0].astype(F32), ssd_conv_b[0].reshape(1, -1).astype(F32),
                                      seq, tm)

    yssd, yret = _mixer_call(xbc, proj, dt_raw, bsz, seq, ssd_dt_bias[0], ssd_A_log[0], ssd_D[0],
                             ssd_norm_w[0], ret_norm_w[0])

    x1, h2 = _merge_call(yssd, yret, proj, x2, mod3, norm_ffn_w.reshape(1, d),
                         w_branch_ssd[0].astype(BF16), w_branch_ret[0].astype(BF16), w_out[0].astype(BF16),
                         seq, tm)

    wg = w_gate_up[0][:, :FFN_HIDDEN].astype(BF16)
    wu = w_gate_up[0][:, FFN_HIDDEN:].astype(BF16)
    out = _ffn_call(h2, x1, mod3, norm_final_w.reshape(1, d), wg, wu, w_down[0].astype(BF16), seq, tm)
    return out.reshape(bsz, seq, d)
```

```python
import functools
import math

import jax
import jax.numpy as jnp
import numpy as np
from jax import lax
from jax.experimental import pallas as pl
from jax.experimental.pallas import tpu as pltpu

F32 = jnp.float32
BF16 = jnp.bfloat16

D_MODEL = 1024
SSD_HEADS = 16
SSD_HEAD_DIM = 64
SSD_D_INNER = SSD_HEADS * SSD_HEAD_DIM
SSD_GROUPS = 4
SSD_HPG = SSD_HEADS // SSD_GROUPS
SSD_D_STATE = 128
SSD_CONV = 4
SSD_BC = SSD_GROUPS * SSD_D_STATE
SSD_CONV_DIM = SSD_D_INNER + 2 * SSD_BC
RET_HEADS = 4
RET_QK_DIM = 128
RET_V_DIM = 256
RET_D_QK = RET_HEADS * RET_QK_DIM
RET_D_V = RET_HEADS * RET_V_DIM
ROPE_BASE = 10000.0
CHUNK = 128
FFN_HIDDEN = 2816
NORM_EPS = 1e-6
IN_SIZES = (SSD_D_INNER, SSD_CONV_DIM, SSD_HEADS, RET_D_QK, RET_D_QK, RET_D_V, RET_D_V, D_MODEL, D_MODEL)

LANES = 128
PROJ_W = 6144
PROJ_BLK = 1024
BLK_Z, BLK_V, BLK_G, BLK_QK, BLK_GA, BLK_GB = 0, 1, 2, 3, 4, 5
MIXER_BLOCKS = 4
GATE_PROJ_ORDER = (BLK_QK, BLK_Z, BLK_G, BLK_GA, BLK_GB, BLK_V)
BLK_XS, BLK_BC = 0, 1
HALO = 16
SILU_BLOCKS = (BLK_Z, BLK_G)
SIGMOID_BLOCKS = (BLK_GA, BLK_GB)
LOG2E = math.log2(math.e)
MASKED_LOG2 = -1e30

VMEM_LIMIT = 56 * 1024 * 1024
TOKEN_TILE = 512


def _silu(v):
    return v * (1.0 / (1.0 + jnp.exp(-v)))


def _sigmoid(v):
    return 1.0 / (1.0 + jnp.exp(-v))


def _dot(a, b):
    return jnp.dot(a, b, preferred_element_type=F32)


def _dot_nt(a, b):
    return lax.dot_general(a, b, (((1,), (1,)), ((), ())), preferred_element_type=F32)


def _dot_tn(a, b):
    return lax.dot_general(a, b, (((0,), (0,)), ((), ())), preferred_element_type=F32)


def _split3(v):
    hi = v.astype(BF16)
    r1 = v - hi.astype(F32)
    mid = r1.astype(BF16)
    lo = (r1 - mid.astype(F32)).astype(BF16)
    return hi, mid, lo


def _resident(shape):
    nd = len(shape)
    return pl.BlockSpec(shape, lambda *_: (0,) * nd, pipeline_mode=pl.Buffered(1))


def _mod_kernel(c_ref, w_ref, b_ref, o_ref):
    cond = _silu(c_ref[...])
    o_ref[...] = jnp.dot(cond, w_ref[...], preferred_element_type=F32,
                         precision=lax.Precision.HIGHEST) + b_ref[...]


def _mod_call(c, w_ada, b_ada):
    bsz, d = c.shape
    n = w_ada.shape[1]
    tn = 1024
    return pl.pallas_call(
        _mod_kernel,
        grid=(n // tn,),
        in_specs=[pl.BlockSpec((bsz, d), lambda j: (0, 0)),
                  pl.BlockSpec((d, tn), lambda j: (0, j)),
                  pl.BlockSpec((1, tn), lambda j: (0, j))],
        out_specs=pl.BlockSpec((bsz, tn), lambda j: (0, j)),
        out_shape=jax.ShapeDtypeStruct((bsz, n), F32),
        name="mod",
    )(c, w_ada, b_ada.reshape(1, n))


def _modulated_norm(x, nw_ref, scale_ref, shift_ref):
    y = x * lax.rsqrt(jnp.mean(x * x, axis=-1, keepdims=True) + NORM_EPS)
    return ((y * nw_ref[...]) * (1.0 + scale_ref[0]) + shift_ref[0]).astype(BF16)


def _conv_proj_kernel(tiles_per_seq, x_ref, halo_ref, shift_ref, scale_ref, nw_ref, w_ref, wdt_ref,
                      convw_ref, convb_ref, xbc_ref, dt_ref, h_ref):
    hb = _modulated_norm(x_ref[...], nw_ref, scale_ref, shift_ref)
    h_ref[...] = hb
    hb_halo = _modulated_norm(halo_ref[...], nw_ref, scale_ref, shift_ref)
    first_tile = (pl.program_id(0) % tiles_per_seq) == 0
    dt_ref[...] = _dot(hb, wdt_ref[...])
    for blk in range(SSD_CONV_DIM // PROJ_BLK):
        cols = slice(blk * PROJ_BLK, (blk + 1) * PROJ_BLK)
        r_halo = jnp.where(first_tile, 0.0, _dot(hb_halo, w_ref[:, cols]))
        xe = jnp.concatenate([r_halo, _dot(hb, w_ref[:, cols])], axis=0)
        acc = convw_ref[0:1, cols] * xe
        for j in range(1, SSD_CONV):
            acc = convw_ref[j:j + 1, cols] * xe + pltpu.roll(acc, 1, 0)
        xbc_ref[:, cols] = _silu(acc[HALO:, :] + convb_ref[:, cols]).astype(BF16)


def _gate_proj_kernel(h_ref, pos_ref, invf_ref, w_ref, proj_ref):
    hb = h_ref[...]
    half = RET_QK_DIM // 2
    ang_t = invf_ref[...] * pos_ref[0]
    cos_t = jnp.cos(ang_t)
    sin_t = jnp.sin(ang_t)
    cos2 = jnp.concatenate([cos_t, cos_t], axis=0).T
    sin2 = jnp.concatenate([-sin_t, sin_t], axis=0).T
    kscale = RET_QK_DIM ** -0.5
    for blk in GATE_PROJ_ORDER:
        cols = slice(blk * PROJ_BLK, (blk + 1) * PROJ_BLK)
        r = _dot(hb, w_ref[:, cols])
        if blk == BLK_QK:
            for hd in range(PROJ_BLK // RET_QK_DIM):
                hcols = slice(hd * RET_QK_DIM, (hd + 1) * RET_QK_DIM)
                t_h = r[:, hcols]
                rot = t_h * cos2 + pltpu.roll(t_h, half, 1) * sin2
                if hd >= RET_HEADS:
                    rot = rot * kscale
                proj_ref[:, blk * PROJ_BLK + hd * RET_QK_DIM:blk * PROJ_BLK + (hd + 1) * RET_QK_DIM] = rot.astype(BF16)
            continue
        if blk in SILU_BLOCKS:
            r = _silu(r)
        elif blk in SIGMOID_BLOCKS:
            r = _sigmoid(r)
        proj_ref[:, cols] = r.astype(BF16)


def _inproj_calls(x2, positions, mod3, norm_w, w_xbc, w_dt, w_main, conv_w, conv_b, seq, tm):
    t, d = x2.shape
    tps = seq // tm
    halo_blocks_per_tile = tm // HALO
    params = pltpu.CompilerParams(dimension_semantics=("arbitrary",), vmem_limit_bytes=VMEM_LIMIT)
    tok = pl.BlockSpec((tm, d), lambda i: (i, 0))
    shift_m = pl.BlockSpec((1, 1, d), lambda i: (i // tps, 0, 0))
    scale_m = pl.BlockSpec((1, 1, d), lambda i: (i // tps, 0, 1))
    xbc, dt_raw, h = pl.pallas_call(
        functools.partial(_conv_proj_kernel, tps),
        grid=(t // tm,),
        in_specs=[tok,
                  pl.BlockSpec((HALO, d), lambda i: (jnp.maximum(i * halo_blocks_per_tile - 1, 0), 0)),
                  shift_m, scale_m, _resident((1, d)), _resident(w_xbc.shape), _resident(w_dt.shape),
                  _resident(conv_w.shape), _resident(conv_b.shape)],
        out_specs=[pl.BlockSpec((tm, SSD_CONV_DIM), lambda i: (i, 0)),
                   pl.BlockSpec((tm, LANES), lambda i: (i, 0)),
                   tok],
        out_shape=[jax.ShapeDtypeStruct((t, SSD_CONV_DIM), BF16),
                   jax.ShapeDtypeStruct((t, LANES), F32),
                   jax.ShapeDtypeStruct((t, d), BF16)],
        compiler_params=params,
        name="conv_proj",
    )(x2, x2, mod3, mod3, norm_w, w_xbc, w_dt, conv_w, conv_b)
    half = RET_QK_DIM // 2
    inv_freq = jnp.power(ROPE_BASE, -jnp.arange(half, dtype=F32) / half)
    invf = jnp.broadcast_to(inv_freq[:, None], (half, tm))
    proj = pl.pallas_call(
        _gate_proj_kernel,
        grid=(t // tm,),
        in_specs=[tok, pl.BlockSpec((1, 1, tm), lambda i: (i, 0, 0)), _resident(invf.shape),
                  _resident(w_main.shape)],
        out_specs=pl.BlockSpec((tm, PROJ_W), lambda i: (i, 0)),
        out_shape=jax.ShapeDtypeStruct((t, PROJ_W), BF16),
        compiler_params=params,
        name="gate_proj",
    )(h, positions.astype(F32).reshape(t // tm, 1, tm), invf, w_main)
    return xbc, dt_raw, proj


def _decay_terms(dt_raw, dtb_ref, alog_ref, tri_ref, hexp_ref, head_ref, wide_ref):
    L = CHUNK
    dtr = dt_raw + dtb_ref[...]
    dt = jnp.maximum(dtr, 0.0) + jnp.log1p(jnp.exp(-jnp.abs(dtr)))
    a2 = dt * (-LOG2E * jnp.exp(alog_ref[...]))
    a_hi, a_mid, a_lo = _split3(a2)
    acum3 = _dot(tri_ref[...], jnp.concatenate([a_hi, a_mid, a_lo], axis=1))
    acum = acum3[:, :LANES] + acum3[:, LANES:2 * LANES] + acum3[:, 2 * LANES:]
    a_last = acum[L - 1:L, :]
    head_ref[0] = acum
    head_ref[1] = (acum - jnp.log2(dt)).T
    per_head = jnp.concatenate([dt * jnp.exp2(a_last - acum), jnp.exp2(acum)], axis=0)
    hi = per_head.astype(BF16)
    mid = (per_head - hi.astype(F32)).astype(BF16)
    spread = _dot(jnp.concatenate([hi, mid], axis=1), hexp_ref[...])
    wide_ref[0] = spread[:L]
    wide_ref[1] = spread[L:]


def _mixer_kernel(xbc_ref, mix_ref, dt_ref, dtn_ref,
                  dtb_ref, alog_ref, dexp_ref, snw_ref, rnw_ref,
                  tri_ref, hexp_ref, dmask_ref, qdec_ref, kdec_ref, cdec_ref,
                  yssd_ref, yret_ref,
                  xbd_ref, sstate_ref, rstate_ref, dhead_ref, dwide_ref):
    L = CHUNK
    P = SSD_HEAD_DIM
    GP = SSD_HPG * P
    xs_ref = xbc_ref.at[:, BLK_XS * PROJ_BLK:(BLK_XS + 1) * PROJ_BLK]
    bc_ref = xbc_ref.at[:, BLK_BC * PROJ_BLK:(BLK_BC + 1) * PROJ_BLK]
    zs_ref = mix_ref.at[:, BLK_Z * PROJ_BLK:(BLK_Z + 1) * PROJ_BLK]
    v_ref = mix_ref.at[:, BLK_V * PROJ_BLK:(BLK_V + 1) * PROJ_BLK]
    gs_ref = mix_ref.at[:, BLK_G * PROJ_BLK:(BLK_G + 1) * PROJ_BLK]
    qk_ref = mix_ref.at[:, BLK_QK * PROJ_BLK:(BLK_QK + 1) * PROJ_BLK]
    step = pl.program_id(0) * pl.num_programs(1) + pl.program_id(1)
    slot = step % 2

    @pl.when(pl.program_id(1) == 0)
    def _():
        xbd_ref[...] = jnp.zeros_like(xbd_ref)
        sstate_ref[...] = jnp.zeros_like(sstate_ref)
        rstate_ref[...] = jnp.zeros_like(rstate_ref)

    @pl.when(step == 0)
    def _():
        _decay_terms(dt_ref[...], dtb_ref, alog_ref, tri_ref, hexp_ref, dhead_ref.at[0], dwide_ref.at[0])

    row = lax.broadcasted_iota(jnp.int32, (L, L), 0)
    col = lax.broadcasted_iota(jnp.int32, (L, L), 1)
    causal = row >= col

    acum = dhead_ref[slot, 0]
    src_t = dhead_ref[slot, 1]
    dtw_x = dwide_ref.at[slot, 0]
    eacum_x = dwide_ref.at[slot, 1]
    xs = xs_ref[...].astype(F32)
    xw = (xs * dtw_x[...]).astype(BF16)
    dskip = dexp_ref[...] * xs

    for g in range(SSD_GROUPS):
        gcols = slice(g * GP, (g + 1) * GP)
        bg = bc_ref[:, g * SSD_D_STATE:(g + 1) * SSD_D_STATE]
        cg = bc_ref[:, SSD_BC + g * SSD_D_STATE:SSD_BC + (g + 1) * SSD_D_STATE]
        cb = _dot_nt(cg, bg)
        ms = []
        for r in range(SSD_HPG):
            h = g * SSD_HPG + r
            seg = acum[:, h:h + 1] - src_t[h:h + 1, :]
            ms.append((cb * jnp.exp2(jnp.where(causal, seg, MASKED_LOG2))).astype(BF16))
            xbd_ref[g, r * L:(r + 1) * L, r * P:(r + 1) * P] = xs_ref[:, g * GP + r * P:g * GP + (r + 1) * P]
        y = _dot(jnp.concatenate(ms, axis=1), xbd_ref[g])
        y = y + _dot(cg, sstate_ref[g].astype(BF16)) * eacum_x[:, gcols]
        sstate_ref[g] = sstate_ref[g] * eacum_x[L - 1:L, gcols] + _dot_tn(bg, xw[:, gcols])
        y = (y + dskip[:, gcols]) * zs_ref[:, gcols].astype(F32)
        y = y * lax.rsqrt(jnp.mean(y * y, axis=-1, keepdims=True) + NORM_EPS)
        yssd_ref[:, gcols] = (y * snw_ref[:, gcols]).astype(BF16)

    for h in range(RET_HEADS):
        qb = qk_ref[:, h * RET_QK_DIM:(h + 1) * RET_QK_DIM]
        kb = qk_ref[:, RET_D_QK + h * RET_QK_DIM:RET_D_QK + (h + 1) * RET_QK_DIM]
        vs = slice(h * RET_V_DIM, (h + 1) * RET_V_DIM)
        vh = v_ref[:, vs]
        scores = _dot_nt(qb, kb) * dmask_ref[h]
        y = _dot(scores.astype(BF16), vh)
        y = y + _dot(qb, rstate_ref[:, vs].astype(BF16)) * qdec_ref[:, vs]
        kd = (kb.astype(F32) * kdec_ref[:, h * RET_QK_DIM:(h + 1) * RET_QK_DIM]).astype(BF16)
        rstate_ref[:, vs] = rstate_ref[:, vs] * cdec_ref[:, vs] + _dot_tn(kd, vh)
        y = y * lax.rsqrt(jnp.mean(y * y, axis=-1, keepdims=True) + NORM_EPS)
        yret_ref[:, vs] = (y * rnw_ref[:, vs] * gs_ref[:, vs].astype(F32)).astype(BF16)

    _decay_terms(dtn_ref[...], dtb_ref, alog_ref, tri_ref, hexp_ref, dhead_ref.at[1 - slot], dwide_ref.at[1 - slot])


def _mixer_consts():
    L = CHUNK
    idx = np.arange(L)
    tri = (idx[:, None] >= idx[None, :]).astype(np.float32)
    hexp = np.zeros((2 * LANES, SSD_D_INNER), np.float32)
    for piece in range(2):
        for h in range(SSD_HEADS):
            hexp[piece * LANES + h, h * SSD_HEAD_DIM:(h + 1) * SSD_HEAD_DIM] = 1.0
    return jnp.asarray(tri, BF16), jnp.asarray(hexp, BF16)


def _retention_tables():
    L = CHUNK
    H = RET_HEADS
    log_gamma = jnp.log1p(-jnp.exp2(-5.0 - jnp.arange(H, dtype=F32)))
    idx = jnp.arange(L, dtype=F32)
    diff = idx[:, None] - idx[None, :]
    dmask = jnp.where(diff >= 0, jnp.exp(jnp.maximum(diff, 0.0)[None] * log_gamma[:, None, None]), 0.0)
    q_decay = jnp.exp((idx + 1.0)[:, None] * log_gamma)
    k_decay = jnp.exp((L - 1.0 - idx)[:, None] * log_gamma)
    chunk_decay = jnp.exp(L * log_gamma)
    qdec = jnp.repeat(q_decay, RET_V_DIM, axis=1)
    kdec = jnp.repeat(k_decay, RET_QK_DIM, axis=1)
    cdec = jnp.repeat(chunk_decay, RET_V_DIM)[None, :]
    return dmask, qdec, kdec, cdec


def _mixer_call(xbc, proj, dt_raw, bsz, seq, dt_bias, a_log, d_skip, ssd_norm_w, ret_norm_w):
    t = proj.shape[0]
    nc = seq // CHUNK
    tri, hexp = _mixer_consts()
    dmask, qdec, kdec, cdec = _retention_tables()

    def pad_heads(v):
        return jnp.zeros((1, LANES), F32).at[0, :SSD_HEADS].set(v.astype(F32))

    def tok(width, blk):
        return pl.BlockSpec((CHUNK, width), lambda b, c: (b * nc + c, blk * PROJ_BLK // width))

    consts = [pad_heads(dt_bias), pad_heads(a_log),
              jnp.repeat(d_skip.astype(F32), SSD_HEAD_DIM)[None, :], ssd_norm_w.reshape(1, -1).astype(F32),
              ret_norm_w.reshape(1, -1).astype(F32), tri, hexp, dmask, qdec, kdec, cdec]
    in_specs = [tok(SSD_CONV_DIM, 0), tok(MIXER_BLOCKS * PROJ_BLK, 0), tok(LANES, 0),
                pl.BlockSpec((CHUNK, LANES), lambda b, c: (jnp.minimum(b * nc + c + 1, bsz * nc - 1), 0))]
    in_specs += [_resident(v.shape) for v in consts]
    return pl.pallas_call(
        _mixer_kernel,
        grid=(bsz, nc),
        in_specs=in_specs,
        out_specs=[pl.BlockSpec((CHUNK, SSD_D_INNER), lambda b, c: (b * nc + c, 0)),
                   pl.BlockSpec((CHUNK, RET_D_V), lambda b, c: (b * nc + c, 0))],
        out_shape=[jax.ShapeDtypeStruct((t, SSD_D_INNER), BF16),
                   jax.ShapeDtypeStruct((t, RET_D_V), BF16)],
        scratch_shapes=[pltpu.VMEM((SSD_GROUPS, SSD_HPG * CHUNK, SSD_HPG * SSD_HEAD_DIM), BF16),
                        pltpu.VMEM((SSD_GROUPS, SSD_D_STATE, SSD_HPG * SSD_HEAD_DIM), F32),
                        pltpu.VMEM((RET_QK_DIM, RET_D_V), F32),
                        pltpu.VMEM((2, 2, CHUNK, LANES), F32),
                        pltpu.VMEM((2, 2, CHUNK, SSD_D_INNER), F32)],
        compiler_params=pltpu.CompilerParams(
            dimension_semantics=("arbitrary", "arbitrary"), vmem_limit_bytes=VMEM_LIMIT),
        name="mixers",
    )(xbc, proj, dt_raw, dt_raw, *consts)


def _merge_kernel(ys_ref, yr_ref, ga_ref, gb_ref, x_ref, gm_ref, shf_ref, scf_ref, nw_ref,
                  wbs_ref, wbr_ref, wo_ref, x1_ref, h2_ref):
    a = _dot(ys_ref[...], wbs_ref[...])
    b = _dot(yr_ref[...], wbr_ref[...])
    merged = ga_ref[...].astype(F32) * a + gb_ref[...].astype(F32) * b
    x1 = x_ref[...] + gm_ref[0] * _dot(merged.astype(BF16), wo_ref[...])
    x1_ref[...] = x1
    y = x1 * lax.rsqrt(jnp.mean(x1 * x1, axis=-1, keepdims=True) + NORM_EPS)
    h2_ref[...] = ((y * nw_ref[...]) * (1.0 + scf_ref[0]) + shf_ref[0]).astype(BF16)


def _merge_call(yssd, yret, proj, x2, mod3, norm_ffn_w, wbs, wbr, wo, seq, tm):
    t, d = x2.shape
    tps = seq // tm

    def tok(blk=0):
        return pl.BlockSpec((tm, d), lambda i: (i, blk))

    def modrow(k):
        return pl.BlockSpec((1, 1, d), lambda i: (i // tps, 0, k))

    return pl.pallas_call(
        _merge_kernel,
        grid=(t // tm,),
        in_specs=[tok(), tok(), tok(BLK_GA), tok(BLK_GB), tok(),
                  modrow(2), modrow(3), modrow(4), _resident((1, d)),
                  _resident(wbs.shape), _resident(wbr.shape), _resident(wo.shape)],
        out_specs=[tok(), tok()],
        out_shape=[jax.ShapeDtypeStruct((t, d), F32), jax.ShapeDtypeStruct((t, d), BF16)],
        compiler_params=pltpu.CompilerParams(
            dimension_semantics=("arbitrary",), vmem_limit_bytes=VMEM_LIMIT),
        name="merge",
    )(yssd, yret, proj, proj, x2, mod3, mod3, mod3, norm_ffn_w, wbs, wbr, wo)


def _ffn_kernel(h2_ref, x1_ref, gf_ref, nw_ref, wg_ref, wu_ref, wd_ref, o_ref):
    h2 = h2_ref[...]
    gate = _dot(h2, wg_ref[...])
    up = _dot(h2, wu_ref[...])
    down = _dot((_silu(gate) * up).astype(BF16), wd_ref[...])
    x2 = x1_ref[...] + gf_ref[0] * down
    y = x2 * lax.rsqrt(jnp.mean(x2 * x2, axis=-1, keepdims=True) + NORM_EPS)
    o_ref[...] = y * nw_ref[...]


def _ffn_call(h2, x1, mod3, norm_final_w, w_gu, wd, seq, tm):
    t, d = x1.shape
    tps = seq // tm
    hidden = wd.shape[0]

    def half(k):
        return pl.BlockSpec((d, hidden), lambda i: (0, k), pipeline_mode=pl.Buffered(1))

    return pl.pallas_call(
        _ffn_kernel,
        grid=(t // tm,),
        in_specs=[pl.BlockSpec((tm, d), lambda i: (i, 0)),
                  pl.BlockSpec((tm, d), lambda i: (i, 0)),
                  pl.BlockSpec((1, 1, d), lambda i: (i // tps, 0, 5)),
                  _resident((1, d)),
                  half(0), half(1), _resident(wd.shape)],
        out_specs=pl.BlockSpec((tm, d), lambda i: (i, 0)),
        out_shape=jax.ShapeDtypeStruct((t, d), F32),
        compiler_params=pltpu.CompilerParams(
            dimension_semantics=("arbitrary",), vmem_limit_bytes=VMEM_LIMIT),
        name="ffn",
    )(h2, x1, mod3, norm_final_w, w_gu, w_gu, wd)


def _token_tile(seq):
    tm = min(TOKEN_TILE, seq)
    assert seq % tm == 0 and tm % CHUNK == 0
    return tm


def kernel(x, c, positions, w_ada, b_ada, norm_mix_w, w_in, ssd_conv_w, ssd_conv_b, ssd_dt_bias, ssd_A_log, ssd_D, ssd_norm_w, ret_norm_w, w_branch_ssd, w_branch_ret, w_out, norm_ffn_w, w_gate_up, w_down, norm_final_w):
    bsz, seq, d = x.shape
    assert d == D_MODEL and seq % CHUNK == 0
    assert w_ada.shape[0] == 1, "single-layer block"
    t = bsz * seq
    tm = _token_tile(seq)
    x2 = x.reshape(t, d)

    mod = _mod_call(c, w_ada[0], b_ada[0])
    mod3 = mod.reshape(bsz, 1, 6 * d)

    w_z, w_xbc, w_dt, w_q, w_k, w_v, w_g, w_ga, w_gb = jnp.split(w_in[0], np.cumsum(IN_SIZES)[:-1].tolist(), axis=1)
    w_main = jnp.concatenate([w_z, w_v, w_g, w_q, w_k, w_ga, w_gb], axis=1).astype(BF16)
    w_dtp = jnp.zeros((d, LANES), BF16).at[:, :SSD_HEADS].set(w_dt.astype(BF16))

    xbc, dt_raw, proj = _inproj_calls(x2, positions, mod3, norm_mix_w.reshape(1, d), w_xbc.astype(BF16), w_dtp,
                                      w_main, ssd_conv_w[0].astype(F32), ssd_conv_b[0].reshape(1, -1).astype(F32),
                                      seq, tm)

    yssd, yret = _mixer_call(xbc, proj, dt_raw, bsz, seq, ssd_dt_bias[0], ssd_A_log[0], ssd_D[0],
                             ssd_norm_w[0], ret_norm_w[0])

    x1, h2 = _merge_call(yssd, yret, proj, x2, mod3, norm_ffn_w.reshape(1, d),
                         w_branch_ssd[0].astype(BF16), w_branch_ret[0].astype(BF16), w_out[0].astype(BF16),
                         seq, tm)

    out = _ffn_call(h2, x1, mod3, norm_final_w.reshape(1, d), w_gate_up[0].astype(BF16),
                    w_down[0].astype(BF16), seq, tm)
    return out.reshape(bsz, seq, d)
```

```python
import functools
import math

import jax
import jax.numpy as jnp
import numpy as np
from jax import lax
from jax.experimental import pallas as pl
from jax.experimental.pallas import tpu as pltpu

F32 = jnp.float32
BF16 = jnp.bfloat16

D_MODEL = 1024
SSD_HEADS = 16
SSD_HEAD_DIM = 64
SSD_D_INNER = SSD_HEADS * SSD_HEAD_DIM
SSD_GROUPS = 4
SSD_HPG = SSD_HEADS // SSD_GROUPS
SSD_D_STATE = 128
SSD_CONV = 4
SSD_BC = SSD_GROUPS * SSD_D_STATE
SSD_CONV_DIM = SSD_D_INNER + 2 * SSD_BC
RET_HEADS = 4
RET_QK_DIM = 128
RET_V_DIM = 256
RET_D_QK = RET_HEADS * RET_QK_DIM
RET_D_V = RET_HEADS * RET_V_DIM
ROPE_BASE = 10000.0
CHUNK = 128
FFN_HIDDEN = 2816
NORM_EPS = 1e-6
IN_SIZES = (SSD_D_INNER, SSD_CONV_DIM, SSD_HEADS, RET_D_QK, RET_D_QK, RET_D_V, RET_D_V, D_MODEL, D_MODEL)

LANES = 128
PROJ_W = 6144
PROJ_BLK = 1024
BLK_Z, BLK_V, BLK_G, BLK_QK, BLK_GA, BLK_GB = 0, 1, 2, 3, 4, 5
MIXER_BLOCKS = 4
CHUNKS_PER_STEP = 2
GATE_PROJ_ORDER = (BLK_QK, BLK_Z, BLK_G, BLK_GA, BLK_GB, BLK_V)
BLK_XS, BLK_BC = 0, 1
HALO = 16
SILU_BLOCKS = (BLK_Z, BLK_G)
SIGMOID_BLOCKS = (BLK_GA, BLK_GB)
LOG2E = math.log2(math.e)
MASKED_LOG2 = -1e30

VMEM_LIMIT = 56 * 1024 * 1024
TOKEN_TILE = 512


def _silu(v):
    return v * (1.0 / (1.0 + jnp.exp(-v)))


def _sigmoid(v):
    return 1.0 / (1.0 + jnp.exp(-v))


def _dot(a, b):
    return jnp.dot(a, b, preferred_element_type=F32)


def _dot_nt(a, b):
    return lax.dot_general(a, b, (((1,), (1,)), ((), ())), preferred_element_type=F32)


def _dot_tn(a, b):
    return lax.dot_general(a, b, (((0,), (0,)), ((), ())), preferred_element_type=F32)


def _split3(v):
    hi = v.astype(BF16)
    r1 = v - hi.astype(F32)
    mid = r1.astype(BF16)
    lo = (r1 - mid.astype(F32)).astype(BF16)
    return hi, mid, lo


def _resident(shape):
    nd = len(shape)
    return pl.BlockSpec(shape, lambda *_: (0,) * nd, pipeline_mode=pl.Buffered(1))


def _mod_kernel(c_ref, w_ref, b_ref, o_ref):
    cond = _silu(c_ref[...])
    o_ref[...] = jnp.dot(cond, w_ref[...], preferred_element_type=F32,
                         precision=lax.Precision.HIGHEST) + b_ref[...]


def _mod_call(c, w_ada, b_ada):
    bsz, d = c.shape
    n = w_ada.shape[1]
    tn = 1024
    return pl.pallas_call(
        _mod_kernel,
        grid=(n // tn,),
        in_specs=[pl.BlockSpec((bsz, d), lambda j: (0, 0)),
                  pl.BlockSpec((d, tn), lambda j: (0, j)),
                  pl.BlockSpec((1, tn), lambda j: (0, j))],
        out_specs=pl.BlockSpec((bsz, tn), lambda j: (0, j)),
        out_shape=jax.ShapeDtypeStruct((bsz, n), F32),
        name="mod",
    )(c, w_ada, b_ada.reshape(1, n))


def _modulated_norm(x, nw_ref, scale_ref, shift_ref):
    y = x * lax.rsqrt(jnp.mean(x * x, axis=-1, keepdims=True) + NORM_EPS)
    return ((y * nw_ref[...]) * (1.0 + scale_ref[0]) + shift_ref[0]).astype(BF16)


def _conv_proj_kernel(tiles_per_seq, x_ref, halo_ref, shift_ref, scale_ref, nw_ref, w_ref, wdt_ref,
                      convw_ref, convb_ref, xbc_ref, dt_ref, h_ref):
    hb = _modulated_norm(x_ref[...], nw_ref, scale_ref, shift_ref)
    h_ref[...] = hb
    hb_halo = _modulated_norm(halo_ref[...], nw_ref, scale_ref, shift_ref)
    first_tile = (pl.program_id(0) % tiles_per_seq) == 0
    dt_ref[...] = _dot(hb, wdt_ref[...])
    for blk in range(SSD_CONV_DIM // PROJ_BLK):
        cols = slice(blk * PROJ_BLK, (blk + 1) * PROJ_BLK)
        r_halo = jnp.where(first_tile, 0.0, _dot(hb_halo, w_ref[:, cols]))
        xe = jnp.concatenate([r_halo, _dot(hb, w_ref[:, cols])], axis=0)
        acc = convw_ref[0:1, cols] * xe
        for j in range(1, SSD_CONV):
            acc = convw_ref[j:j + 1, cols] * xe + pltpu.roll(acc, 1, 0)
        xbc_ref[:, cols] = _silu(acc[HALO:, :] + convb_ref[:, cols]).astype(BF16)


def _gate_proj_kernel(h_ref, pos_ref, invf_ref, w_ref, proj_ref):
    hb = h_ref[...]
    half = RET_QK_DIM // 2
    ang_t = invf_ref[...] * pos_ref[0]
    cos_t = jnp.cos(ang_t)
    sin_t = jnp.sin(ang_t)
    cos2 = jnp.concatenate([cos_t, cos_t], axis=0).T
    sin2 = jnp.concatenate([-sin_t, sin_t], axis=0).T
    kscale = RET_QK_DIM ** -0.5
    for blk in GATE_PROJ_ORDER:
        cols = slice(blk * PROJ_BLK, (blk + 1) * PROJ_BLK)
        r = _dot(hb, w_ref[:, cols])
        if blk == BLK_QK:
            for hd in range(PROJ_BLK // RET_QK_DIM):
                hcols = slice(hd * RET_QK_DIM, (hd + 1) * RET_QK_DIM)
                t_h = r[:, hcols]
                rot = t_h * cos2 + pltpu.roll(t_h, half, 1) * sin2
                if hd >= RET_HEADS:
                    rot = rot * kscale
                proj_ref[:, blk * PROJ_BLK + hd * RET_QK_DIM:blk * PROJ_BLK + (hd + 1) * RET_QK_DIM] = rot.astype(BF16)
            continue
        if blk in SILU_BLOCKS:
            r = _silu(r)
        elif blk in SIGMOID_BLOCKS:
            r = _sigmoid(r)
        proj_ref[:, cols] = r.astype(BF16)


def _inproj_calls(x2, positions, mod3, norm_w, w_xbc, w_dt, w_main, conv_w, conv_b, seq, tm):
    t, d = x2.shape
    tps = seq // tm
    halo_blocks_per_tile = tm // HALO
    params = pltpu.CompilerParams(dimension_semantics=("arbitrary",), vmem_limit_bytes=VMEM_LIMIT)
    tok = pl.BlockSpec((tm, d), lambda i: (i, 0))
    shift_m = pl.BlockSpec((1, 1, d), lambda i: (i // tps, 0, 0))
    scale_m = pl.BlockSpec((1, 1, d), lambda i: (i // tps, 0, 1))
    xbc, dt_raw, h = pl.pallas_call(
        functools.partial(_conv_proj_kernel, tps),
        grid=(t // tm,),
        in_specs=[tok,
                  pl.BlockSpec((HALO, d), lambda i: (jnp.maximum(i * halo_blocks_per_tile - 1, 0), 0)),
                  shift_m, scale_m, _resident((1, d)), _resident(w_xbc.shape), _resident(w_dt.shape),
                  _resident(conv_w.shape), _resident(conv_b.shape)],
        out_specs=[pl.BlockSpec((tm, SSD_CONV_DIM), lambda i: (i, 0)),
                   pl.BlockSpec((tm, LANES), lambda i: (i, 0)),
                   tok],
        out_shape=[jax.ShapeDtypeStruct((t, SSD_CONV_DIM), BF16),
                   jax.ShapeDtypeStruct((t, LANES), F32),
                   jax.ShapeDtypeStruct((t, d), BF16)],
        compiler_params=params,
        name="conv_proj",
    )(x2, x2, mod3, mod3, norm_w, w_xbc, w_dt, conv_w, conv_b)
    half = RET_QK_DIM // 2
    inv_freq = jnp.power(ROPE_BASE, -jnp.arange(half, dtype=F32) / half)
    invf = jnp.broadcast_to(inv_freq[:, None], (half, tm))
    proj = pl.pallas_call(
        _gate_proj_kernel,
        grid=(t // tm,),
        in_specs=[tok, pl.BlockSpec((1, 1, tm), lambda i: (i, 0, 0)), _resident(invf.shape),
                  _resident(w_main.shape)],
        out_specs=pl.BlockSpec((tm, PROJ_W), lambda i: (i, 0)),
        out_shape=jax.ShapeDtypeStruct((t, PROJ_W), BF16),
        compiler_params=params,
        name="gate_proj",
    )(h, positions.astype(F32).reshape(t // tm, 1, tm), invf, w_main)
    return xbc, dt_raw, proj


def _decay_terms(dt_raw, dtb_ref, alog_ref, tri_ref, hexp_ref, head_ref, wide_ref):
    L = CHUNK
    dtr = dt_raw + dtb_ref[...]
    dt = jnp.maximum(dtr, 0.0) + jnp.log1p(jnp.exp(-jnp.abs(dtr)))
    a2 = dt * (-LOG2E * jnp.exp(alog_ref[...]))
    a_hi, a_mid, a_lo = _split3(a2)
    acum3 = _dot(tri_ref[...], jnp.concatenate([a_hi, a_mid, a_lo], axis=1))
    acum = acum3[:, :LANES] + acum3[:, LANES:2 * LANES] + acum3[:, 2 * LANES:]
    a_last = acum[L - 1:L, :]
    head_ref[0] = acum
    head_ref[1] = (acum - jnp.log2(dt)).T
    per_head = jnp.concatenate([dt * jnp.exp2(a_last - acum), jnp.exp2(acum)], axis=0)
    hi = per_head.astype(BF16)
    mid = (per_head - hi.astype(F32)).astype(BF16)
    spread = _dot(jnp.concatenate([hi, mid], axis=1), hexp_ref[...])
    wide_ref[0] = spread[:L]
    wide_ref[1] = spread[L:]


def _mixer_kernel(xbc_ref, mix_ref, dt_ref, dtn_ref,
                  dtb_ref, alog_ref, dexp_ref, snw_ref, rnw_ref,
                  tri_ref, hexp_ref, dmask_ref, qdec_ref, kdec_ref, cdec_ref,
                  yssd_ref, yret_ref,
                  xbd_ref, sstate_ref, rstate_ref, dhead_ref, dwide_ref):
    L = CHUNK
    P = SSD_HEAD_DIM
    GP = SSD_HPG * P
    step = pl.program_id(0) * pl.num_programs(1) + pl.program_id(1)
    slot = step % 2

    @pl.when(pl.program_id(1) == 0)
    def _():
        xbd_ref[...] = jnp.zeros_like(xbd_ref)
        sstate_ref[...] = jnp.zeros_like(sstate_ref)
        rstate_ref[...] = jnp.zeros_like(rstate_ref)

    def decay_terms(src_ref, to_slot):
        for ci in range(CHUNKS_PER_STEP):
            _decay_terms(src_ref[ci * L:(ci + 1) * L, :], dtb_ref, alog_ref, tri_ref, hexp_ref,
                         dhead_ref.at[to_slot, ci], dwide_ref.at[to_slot, ci])

    @pl.when(step == 0)
    def _():
        decay_terms(dt_ref, 0)

    row = lax.broadcasted_iota(jnp.int32, (L, L), 0)
    col = lax.broadcasted_iota(jnp.int32, (L, L), 1)
    causal = row >= col

    for ci in range(CHUNKS_PER_STEP):
        rows = slice(ci * L, (ci + 1) * L)
        xs_ref = xbc_ref.at[rows, BLK_XS * PROJ_BLK:(BLK_XS + 1) * PROJ_BLK]
        bc_ref = xbc_ref.at[rows, BLK_BC * PROJ_BLK:(BLK_BC + 1) * PROJ_BLK]
        zs_ref = mix_ref.at[rows, BLK_Z * PROJ_BLK:(BLK_Z + 1) * PROJ_BLK]
        v_ref = mix_ref.at[rows, BLK_V * PROJ_BLK:(BLK_V + 1) * PROJ_BLK]
        gs_ref = mix_ref.at[rows, BLK_G * PROJ_BLK:(BLK_G + 1) * PROJ_BLK]
        qk_ref = mix_ref.at[rows, BLK_QK * PROJ_BLK:(BLK_QK + 1) * PROJ_BLK]
        acum = dhead_ref[slot, ci, 0]
        src_t = dhead_ref[slot, ci, 1]
        dtw_x = dwide_ref.at[slot, ci, 0]
        eacum_x = dwide_ref.at[slot, ci, 1]
        xs = xs_ref[...].astype(F32)
        xw = (xs * dtw_x[...]).astype(BF16)
        dskip = dexp_ref[...] * xs

        for g in range(SSD_GROUPS):
            gcols = slice(g * GP, (g + 1) * GP)
            bg = bc_ref[:, g * SSD_D_STATE:(g + 1) * SSD_D_STATE]
            cg = bc_ref[:, SSD_BC + g * SSD_D_STATE:SSD_BC + (g + 1) * SSD_D_STATE]
            cb = _dot_nt(cg, bg)
            ms = []
            for r in range(SSD_HPG):
                h = g * SSD_HPG + r
                seg = acum[:, h:h + 1] - src_t[h:h + 1, :]
                ms.append((cb * jnp.exp2(jnp.where(causal, seg, MASKED_LOG2))).astype(BF16))
                xbd_ref[ci, g, r * L:(r + 1) * L, r * P:(r + 1) * P] = xs_ref[:, g * GP + r * P:g * GP + (r + 1) * P]
            y = _dot(jnp.concatenate(ms, axis=1), xbd_ref[ci, g])
            y = y + _dot(cg, sstate_ref[g].astype(BF16)) * eacum_x[:, gcols]
            sstate_ref[g] = sstate_ref[g] * eacum_x[L - 1:L, gcols] + _dot_tn(bg, xw[:, gcols])
            y = (y + dskip[:, gcols]) * zs_ref[:, gcols].astype(F32)
            y = y * lax.rsqrt(jnp.mean(y * y, axis=-1, keepdims=True) + NORM_EPS)
            yssd_ref[rows, gcols] = (y * snw_ref[:, gcols]).astype(BF16)

        for h in range(RET_HEADS):
            qb = qk_ref[:, h * RET_QK_DIM:(h + 1) * RET_QK_DIM]
            kb = qk_ref[:, RET_D_QK + h * RET_QK_DIM:RET_D_QK + (h + 1) * RET_QK_DIM]
            vs = slice(h * RET_V_DIM, (h + 1) * RET_V_DIM)
            vh = v_ref[:, vs]
            scores = _dot_nt(qb, kb) * dmask_ref[h]
            y = _dot(scores.astype(BF16), vh)
            y = y + _dot(qb, rstate_ref[:, vs].astype(BF16)) * qdec_ref[:, vs]
            kd = (kb.astype(F32) * kdec_ref[:, h * RET_QK_DIM:(h + 1) * RET_QK_DIM]).astype(BF16)
            rstate_ref[:, vs] = rstate_ref[:, vs] * cdec_ref[:, vs] + _dot_tn(kd, vh)
            y = y * lax.rsqrt(jnp.mean(y * y, axis=-1, keepdims=True) + NORM_EPS)
            yret_ref[rows, vs] = (y * rnw_ref[:, vs] * gs_ref[:, vs].astype(F32)).astype(BF16)

    decay_terms(dtn_ref, 1 - slot)


def _mixer_consts():
    L = CHUNK
    idx = np.arange(L)
    tri = (idx[:, None] >= idx[None, :]).astype(np.float32)
    hexp = np.zeros((2 * LANES, SSD_D_INNER), np.float32)
    for piece in range(2):
        for h in range(SSD_HEADS):
            hexp[piece * LANES + h, h * SSD_HEAD_DIM:(h + 1) * SSD_HEAD_DIM] = 1.0
    return jnp.asarray(tri, BF16), jnp.asarray(hexp, BF16)


def _retention_tables():
    L = CHUNK
    H = RET_HEADS
    log_gamma = jnp.log1p(-jnp.exp2(-5.0 - jnp.arange(H, dtype=F32)))
    idx = jnp.arange(L, dtype=F32)
    diff = idx[:, None] - idx[None, :]
    dmask = jnp.where(diff >= 0, jnp.exp(jnp.maximum(diff, 0.0)[None] * log_gamma[:, None, None]), 0.0)
    q_decay = jnp.exp((idx + 1.0)[:, None] * log_gamma)
    k_decay = jnp.exp((L - 1.0 - idx)[:, None] * log_gamma)
    chunk_decay = jnp.exp(L * log_gamma)
    qdec = jnp.repeat(q_decay, RET_V_DIM, axis=1)
    kdec = jnp.repeat(k_decay, RET_QK_DIM, axis=1)
    cdec = jnp.repeat(chunk_decay, RET_V_DIM)[None, :]
    return dmask, qdec, kdec, cdec


def _mixer_call(xbc, proj, dt_raw, bsz, seq, dt_bias, a_log, d_skip, ssd_norm_w, ret_norm_w):
    t = proj.shape[0]
    rows = CHUNKS_PER_STEP * CHUNK
    assert seq % rows == 0
    nc = seq // rows
    tri, hexp = _mixer_consts()
    dmask, qdec, kdec, cdec = _retention_tables()

    def pad_heads(v):
        return jnp.zeros((1, LANES), F32).at[0, :SSD_HEADS].set(v.astype(F32))

    def tok(width):
        return pl.BlockSpec((rows, width), lambda b, c: (b * nc + c, 0))

    consts = [pad_heads(dt_bias), pad_heads(a_log),
              jnp.repeat(d_skip.astype(F32), SSD_HEAD_DIM)[None, :], ssd_norm_w.reshape(1, -1).astype(F32),
              ret_norm_w.reshape(1, -1).astype(F32), tri, hexp, dmask, qdec, kdec, cdec]
    in_specs = [tok(SSD_CONV_DIM), tok(MIXER_BLOCKS * PROJ_BLK), tok(LANES),
                pl.BlockSpec((rows, LANES), lambda b, c: (jnp.minimum(b * nc + c + 1, bsz * nc - 1), 0))]
    in_specs += [_resident(v.shape) for v in consts]
    return pl.pallas_call(
        _mixer_kernel,
        grid=(bsz, nc),
        in_specs=in_specs,
        out_specs=[tok(SSD_D_INNER), tok(RET_D_V)],
        out_shape=[jax.ShapeDtypeStruct((t, SSD_D_INNER), BF16),
                   jax.ShapeDtypeStruct((t, RET_D_V), BF16)],
        scratch_shapes=[pltpu.VMEM((CHUNKS_PER_STEP, SSD_GROUPS, SSD_HPG * CHUNK, SSD_HPG * SSD_HEAD_DIM), BF16),
                        pltpu.VMEM((SSD_GROUPS, SSD_D_STATE, SSD_HPG * SSD_HEAD_DIM), F32),
                        pltpu.VMEM((RET_QK_DIM, RET_D_V), F32),
                        pltpu.VMEM((2, CHUNKS_PER_STEP, 2, CHUNK, LANES), F32),
                        pltpu.VMEM((2, CHUNKS_PER_STEP, 2, CHUNK, SSD_D_INNER), F32)],
        compiler_params=pltpu.CompilerParams(
            dimension_semantics=("arbitrary", "arbitrary"), vmem_limit_bytes=VMEM_LIMIT),
        name="mixers",
    )(xbc, proj, dt_raw, dt_raw, *consts)


def _merge_kernel(ys_ref, yr_ref, ga_ref, gb_ref, x_ref, gm_ref, shf_ref, scf_ref, nw_ref,
                  wbs_ref, wbr_ref, wo_ref, x1_ref, h2_ref):
    a = _dot(ys_ref[...], wbs_ref[...])
    b = _dot(yr_ref[...], wbr_ref[...])
    merged = ga_ref[...].astype(F32) * a + gb_ref[...].astype(F32) * b
    x1 = x_ref[...] + gm_ref[0] * _dot(merged.astype(BF16), wo_ref[...])
    x1_ref[...] = x1
    y = x1 * lax.rsqrt(jnp.mean(x1 * x1, axis=-1, keepdims=True) + NORM_EPS)
    h2_ref[...] = ((y * nw_ref[...]) * (1.0 + scf_ref[0]) + shf_ref[0]).astype(BF16)


def _merge_call(yssd, yret, proj, x2, mod3, norm_ffn_w, wbs, wbr, wo, seq, tm):
    t, d = x2.shape
    tps = seq // tm

    def tok(blk=0):
        return pl.BlockSpec((tm, d), lambda i: (i, blk))

    def modrow(k):
        return pl.BlockSpec((1, 1, d), lambda i: (i // tps, 0, k))

    return pl.pallas_call(
        _merge_kernel,
        grid=(t // tm,),
        in_specs=[tok(), tok(), tok(BLK_GA), tok(BLK_GB), tok(),
                  modrow(2), modrow(3), modrow(4), _resident((1, d)),
                  _resident(wbs.shape), _resident(wbr.shape), _resident(wo.shape)],
        out_specs=[tok(), tok()],
        out_shape=[jax.ShapeDtypeStruct((t, d), F32), jax.ShapeDtypeStruct((t, d), BF16)],
        compiler_params=pltpu.CompilerParams(
            dimension_semantics=("arbitrary",), vmem_limit_bytes=VMEM_LIMIT),
        name="merge",
    )(yssd, yret, proj, proj, x2, mod3, mod3, mod3, norm_ffn_w, wbs, wbr, wo)


def _ffn_kernel(h2_ref, x1_ref, gf_ref, nw_ref, wg_ref, wu_ref, wd_ref, o_ref):
    h2 = h2_ref[...]
    gate = _dot(h2, wg_ref[...])
    up = _dot(h2, wu_ref[...])
    down = _dot((_silu(gate) * up).astype(BF16), wd_ref[...])
    x2 = x1_ref[...] + gf_ref[0] * down
    y = x2 * lax.rsqrt(jnp.mean(x2 * x2, axis=-1, keepdims=True) + NORM_EPS)
    o_ref[...] = y * nw_ref[...]


def _ffn_call(h2, x1, mod3, norm_final_w, w_gu, wd, seq, tm):
    t, d = x1.shape
    tps = seq // tm
    hidden = wd.shape[0]

    def half(k):
        return pl.BlockSpec((d, hidden), lambda i: (0, k), pipeline_mode=pl.Buffered(1))

    return pl.pallas_call(
        _ffn_kernel,
        grid=(t // tm,),
        in_specs=[pl.BlockSpec((tm, d), lambda i: (i, 0)),
                  pl.BlockSpec((tm, d), lambda i: (i, 0)),
                  pl.BlockSpec((1, 1, d), lambda i: (i // tps, 0, 5)),
                  _resident((1, d)),
                  half(0), half(1), _resident(wd.shape)],
        out_specs=pl.BlockSpec((tm, d), lambda i: (i, 0)),
        out_shape=jax.ShapeDtypeStruct((t, d), F32),
        compiler_params=pltpu.CompilerParams(
            dimension_semantics=("arbitrary",), vmem_limit_bytes=VMEM_LIMIT),
        name="ffn",
    )(h2, x1, mod3, norm_final_w, w_gu, w_gu, wd)


def _token_tile(seq):
    tm = min(TOKEN_TILE, seq)
    assert seq % tm == 0 and tm % CHUNK == 0
    return tm


def kernel(x, c, positions, w_ada, b_ada, norm_mix_w, w_in, ssd_conv_w, ssd_conv_b, ssd_dt_bias, ssd_A_log, ssd_D, ssd_norm_w, ret_norm_w, w_branch_ssd, w_branch_ret, w_out, norm_ffn_w, w_gate_up, w_down, norm_final_w):
    bsz, seq, d = x.shape
    assert d == D_MODEL and seq % CHUNK == 0
    assert w_ada.shape[0] == 1, "single-layer block"
    t = bsz * seq
    tm = _token_tile(seq)
    x2 = x.reshape(t, d)

    mod = _mod_call(c, w_ada[0], b_ada[0])
    mod3 = mod.reshape(bsz, 1, 6 * d)

    w_z, w_xbc, w_dt, w_q, w_k, w_v, w_g, w_ga, w_gb = jnp.split(w_in[0], np.cumsum(IN_SIZES)[:-1].tolist(), axis=1)
    w_main = jnp.concatenate([w_z, w_v, w_g, w_q, w_k, w_ga, w_gb], axis=1).astype(BF16)
    w_dtp = jnp.zeros((d, LANES), BF16).at[:, :SSD_HEADS].set(w_dt.astype(BF16))

    xbc, dt_raw, proj = _inproj_calls(x2, positions, mod3, norm_mix_w.reshape(1, d), w_xbc.astype(BF16), w_dtp,
                                      w_main, ssd_conv_w[0].astype(F32), ssd_conv_b[0].reshape(1, -1).astype(F32),
                                      seq, tm)

    yssd, yret = _mixer_call(xbc, proj, dt_raw, bsz, seq, ssd_dt_bias[0], ssd_A_log[0], ssd_D[0],
                             ssd_norm_w[0], ret_norm_w[0])

    x1, h2 = _merge_call(yssd, yret, proj, x2, mod3, norm_ffn_w.reshape(1, d),
                         w_branch_ssd[0].astype(BF16), w_branch_ret[0].astype(BF16), w_out[0].astype(BF16),
                         seq, tm)

    out = _ffn_call(h2, x1, mod3, norm_final_w.reshape(1, d), w_gate_up[0].astype(BF16),
                    w_down[0].astype(BF16), seq, tm)
    return out.reshape(bsz, seq, d)
```

```python
import functools
import math

import jax
import jax.numpy as jnp
import numpy as np
from jax import lax
from jax.experimental import pallas as pl
from jax.experimental.pallas import tpu as pltpu

F32 = jnp.float32
BF16 = jnp.bfloat16

D_MODEL = 1024
SSD_HEADS = 16
SSD_HEAD_DIM = 64
SSD_D_INNER = SSD_HEADS * SSD_HEAD_DIM
SSD_GROUPS = 4
SSD_HPG = SSD_HEADS // SSD_GROUPS
SSD_D_STATE = 128
SSD_CONV = 4
SSD_BC = SSD_GROUPS * SSD_D_STATE
SSD_CONV_DIM = SSD_D_INNER + 2 * SSD_BC
RET_HEADS = 4
RET_QK_DIM = 128
RET_V_DIM = 256
RET_D_QK = RET_HEADS * RET_QK_DIM
RET_D_V = RET_HEADS * RET_V_DIM
ROPE_BASE = 10000.0
CHUNK = 128
FFN_HIDDEN = 2816
NORM_EPS = 1e-6
IN_SIZES = (SSD_D_INNER, SSD_CONV_DIM, SSD_HEADS, RET_D_QK, RET_D_QK, RET_D_V, RET_D_V, D_MODEL, D_MODEL)

LANES = 128
PROJ_W = 6144
PROJ_BLK = 1024
BLK_Z, BLK_V, BLK_G, BLK_QK, BLK_GA, BLK_GB = 0, 1, 2, 3, 4, 5
MIXER_BLOCKS = 4
CHUNKS_PER_STEP = 4
GATE_PROJ_ORDER = (BLK_QK, BLK_Z, BLK_G, BLK_GA, BLK_GB, BLK_V)
BLK_XS, BLK_BC = 0, 1
HALO = 16
SILU_BLOCKS = (BLK_Z, BLK_G)
SIGMOID_BLOCKS = (BLK_GA, BLK_GB)
LOG2E = math.log2(math.e)
MASKED_LOG2 = -1e30

VMEM_LIMIT = 56 * 1024 * 1024
TOKEN_TILE = 512


def _silu(v):
    return v * (1.0 / (1.0 + jnp.exp(-v)))


def _sigmoid(v):
    return 1.0 / (1.0 + jnp.exp(-v))


def _dot(a, b):
    return jnp.dot(a, b, preferred_element_type=F32)


def _dot_nt(a, b):
    return lax.dot_general(a, b, (((1,), (1,)), ((), ())), preferred_element_type=F32)


def _dot_tn(a, b):
    return lax.dot_general(a, b, (((0,), (0,)), ((), ())), preferred_element_type=F32)


def _split3(v):
    hi = v.astype(BF16)
    r1 = v - hi.astype(F32)
    mid = r1.astype(BF16)
    lo = (r1 - mid.astype(F32)).astype(BF16)
    return hi, mid, lo


def _resident(shape):
    nd = len(shape)
    return pl.BlockSpec(shape, lambda *_: (0,) * nd, pipeline_mode=pl.Buffered(1))


def _mod_kernel(c_ref, w_ref, b_ref, o_ref):
    cond = _silu(c_ref[...])
    o_ref[...] = jnp.dot(cond, w_ref[...], preferred_element_type=F32,
                         precision=lax.Precision.HIGHEST) + b_ref[...]


def _mod_call(c, w_ada, b_ada):
    bsz, d = c.shape
    n = w_ada.shape[1]
    tn = 1024
    return pl.pallas_call(
        _mod_kernel,
        grid=(n // tn,),
        in_specs=[pl.BlockSpec((bsz, d), lambda j: (0, 0)),
                  pl.BlockSpec((d, tn), lambda j: (0, j)),
                  pl.BlockSpec((1, tn), lambda j: (0, j))],
        out_specs=pl.BlockSpec((bsz, tn), lambda j: (0, j)),
        out_shape=jax.ShapeDtypeStruct((bsz, n), F32),
        name="mod",
    )(c, w_ada, b_ada.reshape(1, n))


def _modulated_norm(x, nw_ref, scale_ref, shift_ref):
    y = x * lax.rsqrt(jnp.mean(x * x, axis=-1, keepdims=True) + NORM_EPS)
    return ((y * nw_ref[...]) * (1.0 + scale_ref[0]) + shift_ref[0]).astype(BF16)


def _conv_proj_kernel(tiles_per_seq, x_ref, halo_ref, shift_ref, scale_ref, nw_ref, w_ref, wdt_ref,
                      convw_ref, convb_ref, xbc_ref, dt_ref, h_ref):
    hb = _modulated_norm(x_ref[...], nw_ref, scale_ref, shift_ref)
    h_ref[...] = hb
    hb_halo = _modulated_norm(halo_ref[...], nw_ref, scale_ref, shift_ref)
    first_tile = (pl.program_id(0) % tiles_per_seq) == 0
    dt_ref[...] = _dot(hb, wdt_ref[...])
    for blk in range(SSD_CONV_DIM // PROJ_BLK):
        cols = slice(blk * PROJ_BLK, (blk + 1) * PROJ_BLK)
        r_halo = jnp.where(first_tile, 0.0, _dot(hb_halo, w_ref[:, cols]))
        xe = jnp.concatenate([r_halo, _dot(hb, w_ref[:, cols])], axis=0)
        acc = convw_ref[0:1, cols] * xe
        for j in range(1, SSD_CONV):
            acc = convw_ref[j:j + 1, cols] * xe + pltpu.roll(acc, 1, 0)
        xbc_ref[:, cols] = _silu(acc[HALO:, :] + convb_ref[:, cols]).astype(BF16)


def _gate_proj_kernel(h_ref, pos_ref, invf_ref, w_ref, proj_ref):
    hb = h_ref[...]
    half = RET_QK_DIM // 2
    ang_t = invf_ref[...] * pos_ref[0]
    cos_t = jnp.cos(ang_t)
    sin_t = jnp.sin(ang_t)
    cos2 = jnp.concatenate([cos_t, cos_t], axis=0).T
    sin2 = jnp.concatenate([-sin_t, sin_t], axis=0).T
    kscale = RET_QK_DIM ** -0.5
    for blk in GATE_PROJ_ORDER:
        cols = slice(blk * PROJ_BLK, (blk + 1) * PROJ_BLK)
        r = _dot(hb, w_ref[:, cols])
        if blk == BLK_QK:
            for hd in range(PROJ_BLK // RET_QK_DIM):
                hcols = slice(hd * RET_QK_DIM, (hd + 1) * RET_QK_DIM)
                t_h = r[:, hcols]
                rot = t_h * cos2 + pltpu.roll(t_h, half, 1) * sin2
                if hd >= RET_HEADS:
                    rot = rot * kscale
                proj_ref[:, blk * PROJ_BLK + hd * RET_QK_DIM:blk * PROJ_BLK + (hd + 1) * RET_QK_DIM] = rot.astype(BF16)
            continue
        if blk in SILU_BLOCKS:
            r = _silu(r)
        elif blk in SIGMOID_BLOCKS:
            r = _sigmoid(r)
        proj_ref[:, cols] = r.astype(BF16)


def _inproj_calls(x2, positions, mod3, norm_w, w_xbc, w_dt, w_main, conv_w, conv_b, seq, tm):
    t, d = x2.shape
    tps = seq // tm
    halo_blocks_per_tile = tm // HALO
    params = pltpu.CompilerParams(dimension_semantics=("arbitrary",), vmem_limit_bytes=VMEM_LIMIT)
    tok = pl.BlockSpec((tm, d), lambda i: (i, 0))
    shift_m = pl.BlockSpec((1, 1, d), lambda i: (i // tps, 0, 0))
    scale_m = pl.BlockSpec((1, 1, d), lambda i: (i // tps, 0, 1))
    xbc, dt_raw, h = pl.pallas_call(
        functools.partial(_conv_proj_kernel, tps),
        grid=(t // tm,),
        in_specs=[tok,
                  pl.BlockSpec((HALO, d), lambda i: (jnp.maximum(i * halo_blocks_per_tile - 1, 0), 0)),
                  shift_m, scale_m, _resident((1, d)), _resident(w_xbc.shape), _resident(w_dt.shape),
                  _resident(conv_w.shape), _resident(conv_b.shape)],
        out_specs=[pl.BlockSpec((tm, SSD_CONV_DIM), lambda i: (i, 0)),
                   pl.BlockSpec((tm, LANES), lambda i: (i, 0)),
                   tok],
        out_shape=[jax.ShapeDtypeStruct((t, SSD_CONV_DIM), BF16),
                   jax.ShapeDtypeStruct((t, LANES), F32),
                   jax.ShapeDtypeStruct((t, d), BF16)],
        compiler_params=params,
        name="conv_proj",
    )(x2, x2, mod3, mod3, norm_w, w_xbc, w_dt, conv_w, conv_b)
    half = RET_QK_DIM // 2
    inv_freq = jnp.power(ROPE_BASE, -jnp.arange(half, dtype=F32) / half)
    invf = jnp.broadcast_to(inv_freq[:, None], (half, tm))
    proj = pl.pallas_call(
        _gate_proj_kernel,
        grid=(t // tm,),
        in_specs=[tok, pl.BlockSpec((1, 1, tm), lambda i: (i, 0, 0)), _resident(invf.shape),
                  _resident(w_main.shape)],
        out_specs=pl.BlockSpec((tm, PROJ_W), lambda i: (i, 0)),
        out_shape=jax.ShapeDtypeStruct((t, PROJ_W), BF16),
        compiler_params=params,
        name="gate_proj",
    )(h, positions.astype(F32).reshape(t // tm, 1, tm), invf, w_main)
    return xbc, dt_raw, proj


def _decay_terms(dt_raw, dtb_ref, alog_ref, tri_ref, hexp_ref, head_ref, wide_ref):
    L = CHUNK
    dtr = dt_raw + dtb_ref[...]
    dt = jnp.maximum(dtr, 0.0) + jnp.log1p(jnp.exp(-jnp.abs(dtr)))
    a2 = dt * (-LOG2E * jnp.exp(alog_ref[...]))
    a_hi, a_mid, a_lo = _split3(a2)
    acum3 = _dot(tri_ref[...], jnp.concatenate([a_hi, a_mid, a_lo], axis=1))
    acum = acum3[:, :LANES] + acum3[:, LANES:2 * LANES] + acum3[:, 2 * LANES:]
    a_last = acum[L - 1:L, :]
    head_ref[0] = acum
    head_ref[1] = (acum - jnp.log2(dt)).T
    per_head = jnp.concatenate([dt * jnp.exp2(a_last - acum), jnp.exp2(acum)], axis=0)
    hi = per_head.astype(BF16)
    mid = (per_head - hi.astype(F32)).astype(BF16)
    spread = _dot(jnp.concatenate([hi, mid], axis=1), hexp_ref[...])
    wide_ref[0] = spread[:L]
    wide_ref[1] = spread[L:]


def _mixer_kernel(xbc_ref, mix_ref, dt_ref, dtn_ref,
                  dtb_ref, alog_ref, dexp_ref, snw_ref, rnw_ref,
                  tri_ref, hexp_ref, dmask_ref, qdec_ref, kdec_ref, cdec_ref,
                  yssd_ref, yret_ref,
                  xbd_ref, sstate_ref, rstate_ref, dhead_ref, dwide_ref):
    L = CHUNK
    P = SSD_HEAD_DIM
    GP = SSD_HPG * P
    step = pl.program_id(0) * pl.num_programs(1) + pl.program_id(1)
    slot = step % 2

    @pl.when(pl.program_id(1) == 0)
    def _():
        xbd_ref[...] = jnp.zeros_like(xbd_ref)
        sstate_ref[...] = jnp.zeros_like(sstate_ref)
        rstate_ref[...] = jnp.zeros_like(rstate_ref)

    def decay_terms(src_ref, to_slot):
        for ci in range(CHUNKS_PER_STEP):
            _decay_terms(src_ref[ci * L:(ci + 1) * L, :], dtb_ref, alog_ref, tri_ref, hexp_ref,
                         dhead_ref.at[to_slot, ci], dwide_ref.at[to_slot, ci])

    @pl.when(step == 0)
    def _():
        decay_terms(dt_ref, 0)

    row = lax.broadcasted_iota(jnp.int32, (L, L), 0)
    col = lax.broadcasted_iota(jnp.int32, (L, L), 1)
    causal = row >= col

    for ci in range(CHUNKS_PER_STEP):
        rows = slice(ci * L, (ci + 1) * L)
        xs_ref = xbc_ref.at[rows, BLK_XS * PROJ_BLK:(BLK_XS + 1) * PROJ_BLK]
        bc_ref = xbc_ref.at[rows, BLK_BC * PROJ_BLK:(BLK_BC + 1) * PROJ_BLK]
        zs_ref = mix_ref.at[rows, BLK_Z * PROJ_BLK:(BLK_Z + 1) * PROJ_BLK]
        v_ref = mix_ref.at[rows, BLK_V * PROJ_BLK:(BLK_V + 1) * PROJ_BLK]
        gs_ref = mix_ref.at[rows, BLK_G * PROJ_BLK:(BLK_G + 1) * PROJ_BLK]
        qk_ref = mix_ref.at[rows, BLK_QK * PROJ_BLK:(BLK_QK + 1) * PROJ_BLK]
        acum = dhead_ref[slot, ci, 0]
        src_t = dhead_ref[slot, ci, 1]
        dtw_x = dwide_ref.at[slot, ci, 0]
        eacum_x = dwide_ref.at[slot, ci, 1]
        xs = xs_ref[...].astype(F32)
        xw = (xs * dtw_x[...]).astype(BF16)
        dskip = dexp_ref[...] * xs

        for g in range(SSD_GROUPS):
            gcols = slice(g * GP, (g + 1) * GP)
            bg = bc_ref[:, g * SSD_D_STATE:(g + 1) * SSD_D_STATE]
            cg = bc_ref[:, SSD_BC + g * SSD_D_STATE:SSD_BC + (g + 1) * SSD_D_STATE]
            cb = _dot_nt(cg, bg)
            ms = []
            for r in range(SSD_HPG):
                h = g * SSD_HPG + r
                seg = acum[:, h:h + 1] - src_t[h:h + 1, :]
                ms.append((cb * jnp.exp2(jnp.where(causal, seg, MASKED_LOG2))).astype(BF16))
                xbd_ref[ci, g, r * L:(r + 1) * L, r * P:(r + 1) * P] = xs_ref[:, g * GP + r * P:g * GP + (r + 1) * P]
            y = _dot(jnp.concatenate(ms, axis=1), xbd_ref[ci, g])
            y = y + _dot(cg, sstate_ref[g].astype(BF16)) * eacum_x[:, gcols]
            sstate_ref[g] = sstate_ref[g] * eacum_x[L - 1:L, gcols] + _dot_tn(bg, xw[:, gcols])
            y = (y + dskip[:, gcols]) * zs_ref[:, gcols].astype(F32)
            y = y * lax.rsqrt(jnp.mean(y * y, axis=-1, keepdims=True) + NORM_EPS)
            yssd_ref[rows, gcols] = (y * snw_ref[:, gcols]).astype(BF16)

        for h in range(RET_HEADS):
            qb = qk_ref[:, h * RET_QK_DIM:(h + 1) * RET_QK_DIM]
            kb = qk_ref[:, RET_D_QK + h * RET_QK_DIM:RET_D_QK + (h + 1) * RET_QK_DIM]
            vs = slice(h * RET_V_DIM, (h + 1) * RET_V_DIM)
            vh = v_ref[:, vs]
            scores = _dot_nt(qb, kb) * dmask_ref[h]
            y = _dot(scores.astype(BF16), vh)
            y = y + _dot(qb, rstate_ref[:, vs].astype(BF16)) * qdec_ref[:, vs]
            kd = (kb.astype(F32) * kdec_ref[:, h * RET_QK_DIM:(h + 1) * RET_QK_DIM]).astype(BF16)
            rstate_ref[:, vs] = rstate_ref[:, vs] * cdec_ref[:, vs] + _dot_tn(kd, vh)
            y = y * lax.rsqrt(jnp.mean(y * y, axis=-1, keepdims=True) + NORM_EPS)
            yret_ref[rows, vs] = (y * rnw_ref[:, vs] * gs_ref[:, vs].astype(F32)).astype(BF16)

    decay_terms(dtn_ref, 1 - slot)


def _mixer_consts():
    L = CHUNK
    idx = np.arange(L)
    tri = (idx[:, None] >= idx[None, :]).astype(np.float32)
    hexp = np.zeros((2 * LANES, SSD_D_INNER), np.float32)
    for piece in range(2):
        for h in range(SSD_HEADS):
            hexp[piece * LANES + h, h * SSD_HEAD_DIM:(h + 1) * SSD_HEAD_DIM] = 1.0
    return jnp.asarray(tri, BF16), jnp.asarray(hexp, BF16)


def _retention_tables():
    L = CHUNK
    H = RET_HEADS
    log_gamma = jnp.log1p(-jnp.exp2(-5.0 - jnp.arange(H, dtype=F32)))
    idx = jnp.arange(L, dtype=F32)
    diff = idx[:, None] - idx[None, :]
    dmask = jnp.where(diff >= 0, jnp.exp(jnp.maximum(diff, 0.0)[None] * log_gamma[:, None, None]), 0.0)
    q_decay = jnp.exp((idx + 1.0)[:, None] * log_gamma)
    k_decay = jnp.exp((L - 1.0 - idx)[:, None] * log_gamma)
    chunk_decay = jnp.exp(L * log_gamma)
    qdec = jnp.repeat(q_decay, RET_V_DIM, axis=1)
    kdec = jnp.repeat(k_decay, RET_QK_DIM, axis=1)
    cdec = jnp.repeat(chunk_decay, RET_V_DIM)[None, :]
    return dmask, qdec, kdec, cdec


def _mixer_call(xbc, proj, dt_raw, bsz, seq, dt_bias, a_log, d_skip, ssd_norm_w, ret_norm_w):
    t = proj.shape[0]
    rows = CHUNKS_PER_STEP * CHUNK
    assert seq % rows == 0
    nc = seq // rows
    tri, hexp = _mixer_consts()
    dmask, qdec, kdec, cdec = _retention_tables()

    def pad_heads(v):
        return jnp.zeros((1, LANES), F32).at[0, :SSD_HEADS].set(v.astype(F32))

    def tok(width):
        return pl.BlockSpec((rows, width), lambda b, c: (b * nc + c, 0))

    consts = [pad_heads(dt_bias), pad_heads(a_log),
              jnp.repeat(d_skip.astype(F32), SSD_HEAD_DIM)[None, :], ssd_norm_w.reshape(1, -1).astype(F32),
              ret_norm_w.reshape(1, -1).astype(F32), tri, hexp, dmask, qdec, kdec, cdec]
    in_specs = [tok(SSD_CONV_DIM), tok(MIXER_BLOCKS * PROJ_BLK), tok(LANES),
                pl.BlockSpec((rows, LANES), lambda b, c: (jnp.minimum(b * nc + c + 1, bsz * nc - 1), 0))]
    in_specs += [_resident(v.shape) for v in consts]
    return pl.pallas_call(
        _mixer_kernel,
        grid=(bsz, nc),
        in_specs=in_specs,
        out_specs=[tok(SSD_D_INNER), tok(RET_D_V)],
        out_shape=[jax.ShapeDtypeStruct((t, SSD_D_INNER), BF16),
                   jax.ShapeDtypeStruct((t, RET_D_V), BF16)],
        scratch_shapes=[pltpu.VMEM((CHUNKS_PER_STEP, SSD_GROUPS, SSD_HPG * CHUNK, SSD_HPG * SSD_HEAD_DIM), BF16),
                        pltpu.VMEM((SSD_GROUPS, SSD_D_STATE, SSD_HPG * SSD_HEAD_DIM), F32),
                        pltpu.VMEM((RET_QK_DIM, RET_D_V), F32),
                        pltpu.VMEM((2, CHUNKS_PER_STEP, 2, CHUNK, LANES), F32),
                        pltpu.VMEM((2, CHUNKS_PER_STEP, 2, CHUNK, SSD_D_INNER), F32)],
        compiler_params=pltpu.CompilerParams(
            dimension_semantics=("arbitrary", "arbitrary"), vmem_limit_bytes=VMEM_LIMIT),
        name="mixers",
    )(xbc, proj, dt_raw, dt_raw, *consts)


def _merge_kernel(ys_ref, yr_ref, ga_ref, gb_ref, x_ref, gm_ref, shf_ref, scf_ref, nw_ref,
                  wbs_ref, wbr_ref, wo_ref, x1_ref, h2_ref):
    a = _dot(ys_ref[...], wbs_ref[...])
    b = _dot(yr_ref[...], wbr_ref[...])
    merged = ga_ref[...].astype(F32) * a + gb_ref[...].astype(F32) * b
    x1 = x_ref[...] + gm_ref[0] * _dot(merged.astype(BF16), wo_ref[...])
    x1_ref[...] = x1
    y = x1 * lax.rsqrt(jnp.mean(x1 * x1, axis=-1, keepdims=True) + NORM_EPS)
    h2_ref[...] = ((y * nw_ref[...]) * (1.0 + scf_ref[0]) + shf_ref[0]).astype(BF16)


def _merge_call(yssd, yret, proj, x2, mod3, norm_ffn_w, wbs, wbr, wo, seq, tm):
    t, d = x2.shape
    tps = seq // tm

    def tok(blk=0):
        return pl.BlockSpec((tm, d), lambda i: (i, blk))

    def modrow(k):
        return pl.BlockSpec((1, 1, d), lambda i: (i // tps, 0, k))

    return pl.pallas_call(
        _merge_kernel,
        grid=(t // tm,),
        in_specs=[tok(), tok(), tok(BLK_GA), tok(BLK_GB), tok(),
                  modrow(2), modrow(3), modrow(4), _resident((1, d)),
                  _resident(wbs.shape), _resident(wbr.shape), _resident(wo.shape)],
        out_specs=[tok(), tok()],
        out_shape=[jax.ShapeDtypeStruct((t, d), F32), jax.ShapeDtypeStruct((t, d), BF16)],
        compiler_params=pltpu.CompilerParams(
            dimension_semantics=("arbitrary",), vmem_limit_bytes=VMEM_LIMIT),
        name="merge",
    )(yssd, yret, proj, proj, x2, mod3, mod3, mod3, norm_ffn_w, wbs, wbr, wo)


def _ffn_kernel(h2_ref, x1_ref, gf_ref, nw_ref, wg_ref, wu_ref, wd_ref, o_ref):
    h2 = h2_ref[...]
    gate = _dot(h2, wg_ref[...])
    up = _dot(h2, wu_ref[...])
    down = _dot((_silu(gate) * up).astype(BF16), wd_ref[...])
    x2 = x1_ref[...] + gf_ref[0] * down
    y = x2 * lax.rsqrt(jnp.mean(x2 * x2, axis=-1, keepdims=True) + NORM_EPS)
    o_ref[...] = y * nw_ref[...]


def _ffn_call(h2, x1, mod3, norm_final_w, w_gu, wd, seq, tm):
    t, d = x1.shape
    tps = seq // tm
    hidden = wd.shape[0]

    def half(k):
        return pl.BlockSpec((d, hidden), lambda i: (0, k), pipeline_mode=pl.Buffered(1))

    return pl.pallas_call(
        _ffn_kernel,
        grid=(t // tm,),
        in_specs=[pl.BlockSpec((tm, d), lambda i: (i, 0)),
                  pl.BlockSpec((tm, d), lambda i: (i, 0)),
                  pl.BlockSpec((1, 1, d), lambda i: (i // tps, 0, 5)),
                  _resident((1, d)),
                  half(0), half(1), _resident(wd.shape)],
        out_specs=pl.BlockSpec((tm, d), lambda i: (i, 0)),
        out_shape=jax.ShapeDtypeStruct((t, d), F32),
        compiler_params=pltpu.CompilerParams(
            dimension_semantics=("arbitrary",), vmem_limit_bytes=VMEM_LIMIT),
        name="ffn",
    )(h2, x1, mod3, norm_final_w, w_gu, w_gu, wd)


def _token_tile(seq):
    tm = min(TOKEN_TILE, seq)
    assert seq % tm == 0 and tm % CHUNK == 0
    return tm


def kernel(x, c, positions, w_ada, b_ada, norm_mix_w, w_in, ssd_conv_w, ssd_conv_b, ssd_dt_bias, ssd_A_log, ssd_D, ssd_norm_w, ret_norm_w, w_branch_ssd, w_branch_ret, w_out, norm_ffn_w, w_gate_up, w_down, norm_final_w):
    bsz, seq, d = x.shape
    assert d == D_MODEL and seq % CHUNK == 0
    assert w_ada.shape[0] == 1, "single-layer block"
    t = bsz * seq
    tm = _token_tile(seq)
    x2 = x.reshape(t, d)

    mod = _mod_call(c, w_ada[0], b_ada[0])
    mod3 = mod.reshape(bsz, 1, 6 * d)

    w_z, w_xbc, w_dt, w_q, w_k, w_v, w_g, w_ga, w_gb = jnp.split(w_in[0], np.cumsum(IN_SIZES)[:-1].tolist(), axis=1)
    w_main = jnp.concatenate([w_z, w_v, w_g, w_q, w_k, w_ga, w_gb], axis=1).astype(BF16)
    w_dtp = jnp.zeros((d, LANES), BF16).at[:, :SSD_HEADS].set(w_dt.astype(BF16))

    xbc, dt_raw, proj = _inproj_calls(x2, positions, mod3, norm_mix_w.reshape(1, d), w_xbc.astype(BF16), w_dtp,
                                      w_main, ssd_conv_w[0].astype(F32), ssd_conv_b[0].reshape(1, -1).astype(F32),
                                      seq, tm)

    yssd, yret = _mixer_call(xbc, proj, dt_raw, bsz, seq, ssd_dt_bias[0], ssd_A_log[0], ssd_D[0],
                             ssd_norm_w[0], ret_norm_w[0])

    x1, h2 = _merge_call(yssd, yret, proj, x2, mod3, norm_ffn_w.reshape(1, d),
                         w_branch_ssd[0].astype(BF16), w_branch_ret[0].astype(BF16), w_out[0].astype(BF16),
                         seq, tm)

    out = _ffn_call(h2, x1, mod3, norm_final_w.reshape(1, d), w_gate_up[0].astype(BF16),
                    w_down[0].astype(BF16), seq, tm)
    return out.reshape(bsz, seq, d)
```

```python
import functools
import math

import jax
import jax.numpy as jnp
import numpy as np
from jax import lax
from jax.experimental import pallas as pl
from jax.experimental.pallas import tpu as pltpu

F32 = jnp.float32
BF16 = jnp.bfloat16

D_MODEL = 1024
SSD_HEADS = 16
SSD_HEAD_DIM = 64
SSD_D_INNER = SSD_HEADS * SSD_HEAD_DIM
SSD_GROUPS = 4
SSD_HPG = SSD_HEADS // SSD_GROUPS
SSD_D_STATE = 128
SSD_CONV = 4
SSD_BC = SSD_GROUPS * SSD_D_STATE
SSD_CONV_DIM = SSD_D_INNER + 2 * SSD_BC
RET_HEADS = 4
RET_QK_DIM = 128
RET_V_DIM = 256
RET_D_QK = RET_HEADS * RET_QK_DIM
RET_D_V = RET_HEADS * RET_V_DIM
ROPE_BASE = 10000.0
CHUNK = 128
FFN_HIDDEN = 2816
NORM_EPS = 1e-6
IN_SIZES = (SSD_D_INNER, SSD_CONV_DIM, SSD_HEADS, RET_D_QK, RET_D_QK, RET_D_V, RET_D_V, D_MODEL, D_MODEL)

LANES = 128
PROJ_W = 6144
PROJ_BLK = 1024
BLK_Z, BLK_V, BLK_G, BLK_QK, BLK_GA, BLK_GB = 0, 1, 2, 3, 4, 5
MIXER_BLOCKS = 4
CHUNKS_PER_STEP = 4
GATE_PROJ_ORDER = (BLK_QK, BLK_Z, BLK_G, BLK_GA, BLK_GB, BLK_V)
BLK_XS, BLK_BC = 0, 1
HALO = 16
SILU_BLOCKS = (BLK_Z, BLK_G)
SIGMOID_BLOCKS = (BLK_GA, BLK_GB)
LOG2E = math.log2(math.e)
MASKED_LOG2 = -1e30

VMEM_LIMIT = 56 * 1024 * 1024
TOKEN_TILE = 512
GATE_TOKEN_TILE = 1024


def _silu(v):
    return v * (1.0 / (1.0 + jnp.exp(-v)))


def _sigmoid(v):
    return 1.0 / (1.0 + jnp.exp(-v))


def _dot(a, b):
    return jnp.dot(a, b, preferred_element_type=F32)


def _dot_nt(a, b):
    return lax.dot_general(a, b, (((1,), (1,)), ((), ())), preferred_element_type=F32)


def _dot_tn(a, b):
    return lax.dot_general(a, b, (((0,), (0,)), ((), ())), preferred_element_type=F32)


def _split3(v):
    hi = v.astype(BF16)
    r1 = v - hi.astype(F32)
    mid = r1.astype(BF16)
    lo = (r1 - mid.astype(F32)).astype(BF16)
    return hi, mid, lo


def _resident(shape):
    nd = len(shape)
    return pl.BlockSpec(shape, lambda *_: (0,) * nd, pipeline_mode=pl.Buffered(1))


def _mod_kernel(c_ref, w_ref, b_ref, o_ref):
    cond = _silu(c_ref[...])
    o_ref[...] = jnp.dot(cond, w_ref[...], preferred_element_type=F32,
                         precision=lax.Precision.HIGHEST) + b_ref[...]


def _mod_call(c, w_ada, b_ada):
    bsz, d = c.shape
    n = w_ada.shape[1]
    tn = 1024
    return pl.pallas_call(
        _mod_kernel,
        grid=(n // tn,),
        in_specs=[pl.BlockSpec((bsz, d), lambda j: (0, 0)),
                  pl.BlockSpec((d, tn), lambda j: (0, j)),
                  pl.BlockSpec((1, tn), lambda j: (0, j))],
        out_specs=pl.BlockSpec((bsz, tn), lambda j: (0, j)),
        out_shape=jax.ShapeDtypeStruct((bsz, n), F32),
        name="mod",
    )(c, w_ada, b_ada.reshape(1, n))


def _modulated_norm(x, nw_ref, scale_ref, shift_ref):
    y = x * lax.rsqrt(jnp.mean(x * x, axis=-1, keepdims=True) + NORM_EPS)
    return ((y * nw_ref[...]) * (1.0 + scale_ref[0]) + shift_ref[0]).astype(BF16)


def _conv_proj_kernel(tiles_per_seq, x_ref, halo_ref, shift_ref, scale_ref, nw_ref, w_ref, wdt_ref,
                      convw_ref, convb_ref, xbc_ref, dt_ref, h_ref):
    hb = _modulated_norm(x_ref[...], nw_ref, scale_ref, shift_ref)
    h_ref[...] = hb
    hb_halo = _modulated_norm(halo_ref[...], nw_ref, scale_ref, shift_ref)
    first_tile = (pl.program_id(0) % tiles_per_seq) == 0
    dt_ref[...] = _dot(hb, wdt_ref[...])
    for blk in range(SSD_CONV_DIM // PROJ_BLK):
        cols = slice(blk * PROJ_BLK, (blk + 1) * PROJ_BLK)
        r_halo = jnp.where(first_tile, 0.0, _dot(hb_halo, w_ref[:, cols]))
        xe = jnp.concatenate([r_halo, _dot(hb, w_ref[:, cols])], axis=0)
        acc = convw_ref[0:1, cols] * xe
        for j in range(1, SSD_CONV):
            acc = convw_ref[j:j + 1, cols] * xe + pltpu.roll(acc, 1, 0)
        xbc_ref[:, cols] = _silu(acc[HALO:, :] + convb_ref[:, cols]).astype(BF16)


def _gate_proj_kernel(h_ref, pos_ref, invf_ref, w_ref, proj_ref):
    hb = h_ref[...]
    half = RET_QK_DIM // 2
    ang_t = invf_ref[...] * pos_ref[0]
    cos_t = jnp.cos(ang_t)
    sin_t = jnp.sin(ang_t)
    cos2 = jnp.concatenate([cos_t, cos_t], axis=0).T
    sin2 = jnp.concatenate([-sin_t, sin_t], axis=0).T
    kscale = RET_QK_DIM ** -0.5
    for blk in GATE_PROJ_ORDER:
        cols = slice(blk * PROJ_BLK, (blk + 1) * PROJ_BLK)
        r = _dot(hb, w_ref[:, cols])
        if blk == BLK_QK:
            for hd in range(PROJ_BLK // RET_QK_DIM):
                hcols = slice(hd * RET_QK_DIM, (hd + 1) * RET_QK_DIM)
                t_h = r[:, hcols]
                rot = t_h * cos2 + pltpu.roll(t_h, half, 1) * sin2
                if hd >= RET_HEADS:
                    rot = rot * kscale
                proj_ref[:, blk * PROJ_BLK + hd * RET_QK_DIM:blk * PROJ_BLK + (hd + 1) * RET_QK_DIM] = rot.astype(BF16)
            continue
        if blk in SILU_BLOCKS:
            r = _silu(r)
        elif blk in SIGMOID_BLOCKS:
            r = _sigmoid(r)
        proj_ref[:, cols] = r.astype(BF16)


def _inproj_calls(x2, positions, mod3, norm_w, w_xbc, w_dt, w_main, conv_w, conv_b, seq, tm):
    t, d = x2.shape
    tps = seq // tm
    halo_blocks_per_tile = tm // HALO
    params = pltpu.CompilerParams(dimension_semantics=("arbitrary",), vmem_limit_bytes=VMEM_LIMIT)
    tok = pl.BlockSpec((tm, d), lambda i: (i, 0))
    shift_m = pl.BlockSpec((1, 1, d), lambda i: (i // tps, 0, 0))
    scale_m = pl.BlockSpec((1, 1, d), lambda i: (i // tps, 0, 1))
    xbc, dt_raw, h = pl.pallas_call(
        functools.partial(_conv_proj_kernel, tps),
        grid=(t // tm,),
        in_specs=[tok,
                  pl.BlockSpec((HALO, d), lambda i: (jnp.maximum(i * halo_blocks_per_tile - 1, 0), 0)),
                  shift_m, scale_m, _resident((1, d)), _resident(w_xbc.shape), _resident(w_dt.shape),
                  _resident(conv_w.shape), _resident(conv_b.shape)],
        out_specs=[pl.BlockSpec((tm, SSD_CONV_DIM), lambda i: (i, 0)),
                   pl.BlockSpec((tm, LANES), lambda i: (i, 0)),
                   tok],
        out_shape=[jax.ShapeDtypeStruct((t, SSD_CONV_DIM), BF16),
                   jax.ShapeDtypeStruct((t, LANES), F32),
                   jax.ShapeDtypeStruct((t, d), BF16)],
        compiler_params=params,
        name="conv_proj",
    )(x2, x2, mod3, mod3, norm_w, w_xbc, w_dt, conv_w, conv_b)
    half = RET_QK_DIM // 2
    inv_freq = jnp.power(ROPE_BASE, -jnp.arange(half, dtype=F32) / half)
    tg = min(GATE_TOKEN_TILE, seq)
    invf = jnp.broadcast_to(inv_freq[:, None], (half, tg))
    proj = pl.pallas_call(
        _gate_proj_kernel,
        grid=(t // tg,),
        in_specs=[pl.BlockSpec((tg, d), lambda i: (i, 0)), pl.BlockSpec((1, 1, tg), lambda i: (i, 0, 0)),
                  _resident(invf.shape), _resident(w_main.shape)],
        out_specs=pl.BlockSpec((tg, PROJ_W), lambda i: (i, 0)),
        out_shape=jax.ShapeDtypeStruct((t, PROJ_W), BF16),
        compiler_params=params,
        name="gate_proj",
    )(h, positions.astype(F32).reshape(t // tg, 1, tg), invf, w_main)
    return xbc, dt_raw, proj


def _decay_terms(dt_raw, dtb_ref, alog_ref, tri_ref, hexp_ref, head_ref, wide_ref):
    L = CHUNK
    dtr = dt_raw + dtb_ref[...]
    dt = jnp.maximum(dtr, 0.0) + jnp.log1p(jnp.exp(-jnp.abs(dtr)))
    a2 = dt * (-LOG2E * jnp.exp(alog_ref[...]))
    a_hi, a_mid, a_lo = _split3(a2)
    acum3 = _dot(tri_ref[...], jnp.concatenate([a_hi, a_mid, a_lo], axis=1))
    acum = acum3[:, :LANES] + acum3[:, LANES:2 * LANES] + acum3[:, 2 * LANES:]
    a_last = acum[L - 1:L, :]
    head_ref[0] = acum
    head_ref[1] = (acum - jnp.log2(dt)).T
    per_head = jnp.concatenate([dt * jnp.exp2(a_last - acum), jnp.exp2(acum)], axis=0)
    hi = per_head.astype(BF16)
    mid = (per_head - hi.astype(F32)).astype(BF16)
    spread = _dot(jnp.concatenate([hi, mid], axis=1), hexp_ref[...])
    wide_ref[0] = spread[:L]
    wide_ref[1] = spread[L:]


def _mixer_kernel(xbc_ref, mix_ref, dt_ref, dtn_ref,
                  dtb_ref, alog_ref, dexp_ref, snw_ref, rnw_ref,
                  tri_ref, hexp_ref, dmask_ref, qdec_ref, kdec_ref, cdec_ref,
                  yssd_ref, yret_ref,
                  xbd_ref, sstate_ref, rstate_ref, dhead_ref, dwide_ref):
    L = CHUNK
    P = SSD_HEAD_DIM
    GP = SSD_HPG * P
    step = pl.program_id(0) * pl.num_programs(1) + pl.program_id(1)
    slot = step % 2

    @pl.when(pl.program_id(1) == 0)
    def _():
        xbd_ref[...] = jnp.zeros_like(xbd_ref)
        sstate_ref[...] = jnp.zeros_like(sstate_ref)
        rstate_ref[...] = jnp.zeros_like(rstate_ref)

    def decay_terms(src_ref, to_slot):
        for ci in range(CHUNKS_PER_STEP):
            _decay_terms(src_ref[ci * L:(ci + 1) * L, :], dtb_ref, alog_ref, tri_ref, hexp_ref,
                         dhead_ref.at[to_slot, ci], dwide_ref.at[to_slot, ci])

    @pl.when(step == 0)
    def _():
        decay_terms(dt_ref, 0)

    row = lax.broadcasted_iota(jnp.int32, (L, L), 0)
    col = lax.broadcasted_iota(jnp.int32, (L, L), 1)
    causal = row >= col

    for ci in range(CHUNKS_PER_STEP):
        rows = slice(ci * L, (ci + 1) * L)
        xs_ref = xbc_ref.at[rows, BLK_XS * PROJ_BLK:(BLK_XS + 1) * PROJ_BLK]
        bc_ref = xbc_ref.at[rows, BLK_BC * PROJ_BLK:(BLK_BC + 1) * PROJ_BLK]
        zs_ref = mix_ref.at[rows, BLK_Z * PROJ_BLK:(BLK_Z + 1) * PROJ_BLK]
        v_ref = mix_ref.at[rows, BLK_V * PROJ_BLK:(BLK_V + 1) * PROJ_BLK]
        gs_ref = mix_ref.at[rows, BLK_G * PROJ_BLK:(BLK_G + 1) * PROJ_BLK]
        qk_ref = mix_ref.at[rows, BLK_QK * PROJ_BLK:(BLK_QK + 1) * PROJ_BLK]
        acum = dhead_ref[slot, ci, 0]
        src_t = dhead_ref[slot, ci, 1]
        dtw_x = dwide_ref.at[slot, ci, 0]
        eacum_x = dwide_ref.at[slot, ci, 1]
        xs = xs_ref[...].astype(F32)
        xw = (xs * dtw_x[...]).astype(BF16)
        dskip = dexp_ref[...] * xs

        for g in range(SSD_GROUPS):
            gcols = slice(g * GP, (g + 1) * GP)
            bg = bc_ref[:, g * SSD_D_STATE:(g + 1) * SSD_D_STATE]
            cg = bc_ref[:, SSD_BC + g * SSD_D_STATE:SSD_BC + (g + 1) * SSD_D_STATE]
            cb = _dot_nt(cg, bg)
            ms = []
            for r in range(SSD_HPG):
                h = g * SSD_HPG + r
                seg = acum[:, h:h + 1] - src_t[h:h + 1, :]
                ms.append((cb * jnp.exp2(jnp.where(causal, seg, MASKED_LOG2))).astype(BF16))
                xbd_ref[ci, g, r * L:(r + 1) * L, r * P:(r + 1) * P] = xs_ref[:, g * GP + r * P:g * GP + (r + 1) * P]
            y = _dot(jnp.concatenate(ms, axis=1), xbd_ref[ci, g])
            y = y + _dot(cg, sstate_ref[g].astype(BF16)) * eacum_x[:, gcols]
            sstate_ref[g] = sstate_ref[g] * eacum_x[L - 1:L, gcols] + _dot_tn(bg, xw[:, gcols])
            y = (y + dskip[:, gcols]) * zs_ref[:, gcols].astype(F32)
            y = y * lax.rsqrt(jnp.mean(y * y, axis=-1, keepdims=True) + NORM_EPS)
            yssd_ref[rows, gcols] = (y * snw_ref[:, gcols]).astype(BF16)

        for h in range(RET_HEADS):
            qb = qk_ref[:, h * RET_QK_DIM:(h + 1) * RET_QK_DIM]
            kb = qk_ref[:, RET_D_QK + h * RET_QK_DIM:RET_D_QK + (h + 1) * RET_QK_DIM]
            vs = slice(h * RET_V_DIM, (h + 1) * RET_V_DIM)
            vh = v_ref[:, vs]
            scores = _dot_nt(qb, kb) * dmask_ref[h]
            y = _dot(scores.astype(BF16), vh)
            y = y + _dot(qb, rstate_ref[:, vs].astype(BF16)) * qdec_ref[:, vs]
            kd = (kb.astype(F32) * kdec_ref[:, h * RET_QK_DIM:(h + 1) * RET_QK_DIM]).astype(BF16)
            rstate_ref[:, vs] = rstate_ref[:, vs] * cdec_ref[:, vs] + _dot_tn(kd, vh)
            y = y * lax.rsqrt(jnp.mean(y * y, axis=-1, keepdims=True) + NORM_EPS)
            yret_ref[rows, vs] = (y * rnw_ref[:, vs] * gs_ref[:, vs].astype(F32)).astype(BF16)

    decay_terms(dtn_ref, 1 - slot)


def _mixer_consts():
    L = CHUNK
    idx = np.arange(L)
    tri = (idx[:, None] >= idx[None, :]).astype(np.float32)
    hexp = np.zeros((2 * LANES, SSD_D_INNER), np.float32)
    for piece in range(2):
        for h in range(SSD_HEADS):
            hexp[piece * LANES + h, h * SSD_HEAD_DIM:(h + 1) * SSD_HEAD_DIM] = 1.0
    return jnp.asarray(tri, BF16), jnp.asarray(hexp, BF16)


def _retention_tables():
    L = CHUNK
    H = RET_HEADS
    log_gamma = jnp.log1p(-jnp.exp2(-5.0 - jnp.arange(H, dtype=F32)))
    idx = jnp.arange(L, dtype=F32)
    diff = idx[:, None] - idx[None, :]
    dmask = jnp.where(diff >= 0, jnp.exp(jnp.maximum(diff, 0.0)[None] * log_gamma[:, None, None]), 0.0)
    q_decay = jnp.exp((idx + 1.0)[:, None] * log_gamma)
    k_decay = jnp.exp((L - 1.0 - idx)[:, None] * log_gamma)
    chunk_decay = jnp.exp(L * log_gamma)
    qdec = jnp.repeat(q_decay, RET_V_DIM, axis=1)
    kdec = jnp.repeat(k_decay, RET_QK_DIM, axis=1)
    cdec = jnp.repeat(chunk_decay, RET_V_DIM)[None, :]
    return dmask, qdec, kdec, cdec


def _mixer_call(xbc, proj, dt_raw, bsz, seq, dt_bias, a_log, d_skip, ssd_norm_w, ret_norm_w):
    t = proj.shape[0]
    rows = CHUNKS_PER_STEP * CHUNK
    assert seq % rows == 0
    nc = seq // rows
    tri, hexp = _mixer_consts()
    dmask, qdec, kdec, cdec = _retention_tables()

    def pad_heads(v):
        return jnp.zeros((1, LANES), F32).at[0, :SSD_HEADS].set(v.astype(F32))

    def tok(width):
        return pl.BlockSpec((rows, width), lambda b, c: (b * nc + c, 0))

    consts = [pad_heads(dt_bias), pad_heads(a_log),
              jnp.repeat(d_skip.astype(F32), SSD_HEAD_DIM)[None, :], ssd_norm_w.reshape(1, -1).astype(F32),
              ret_norm_w.reshape(1, -1).astype(F32), tri, hexp, dmask, qdec, kdec, cdec]
    in_specs = [tok(SSD_CONV_DIM), tok(MIXER_BLOCKS * PROJ_BLK), tok(LANES),
                pl.BlockSpec((rows, LANES), lambda b, c: (jnp.minimum(b * nc + c + 1, bsz * nc - 1), 0))]
    in_specs += [_resident(v.shape) for v in consts]
    return pl.pallas_call(
        _mixer_kernel,
        grid=(bsz, nc),
        in_specs=in_specs,
        out_specs=[tok(SSD_D_INNER), tok(RET_D_V)],
        out_shape=[jax.ShapeDtypeStruct((t, SSD_D_INNER), BF16),
                   jax.ShapeDtypeStruct((t, RET_D_V), BF16)],
        scratch_shapes=[pltpu.VMEM((CHUNKS_PER_STEP, SSD_GROUPS, SSD_HPG * CHUNK, SSD_HPG * SSD_HEAD_DIM), BF16),
                        pltpu.VMEM((SSD_GROUPS, SSD_D_STATE, SSD_HPG * SSD_HEAD_DIM), F32),
                        pltpu.VMEM((RET_QK_DIM, RET_D_V), F32),
                        pltpu.VMEM((2, CHUNKS_PER_STEP, 2, CHUNK, LANES), F32),
                        pltpu.VMEM((2, CHUNKS_PER_STEP, 2, CHUNK, SSD_D_INNER), F32)],
        compiler_params=pltpu.CompilerParams(
            dimension_semantics=("arbitrary", "arbitrary"), vmem_limit_bytes=VMEM_LIMIT),
        name="mixers",
    )(xbc, proj, dt_raw, dt_raw, *consts)


def _merge_kernel(ys_ref, yr_ref, ga_ref, gb_ref, x_ref, gm_ref, shf_ref, scf_ref, nw_ref,
                  wbs_ref, wbr_ref, wo_ref, x1_ref, h2_ref):
    a = _dot(ys_ref[...], wbs_ref[...])
    b = _dot(yr_ref[...], wbr_ref[...])
    merged = ga_ref[...].astype(F32) * a + gb_ref[...].astype(F32) * b
    x1 = x_ref[...] + gm_ref[0] * _dot(merged.astype(BF16), wo_ref[...])
    x1_ref[...] = x1
    y = x1 * lax.rsqrt(jnp.mean(x1 * x1, axis=-1, keepdims=True) + NORM_EPS)
    h2_ref[...] = ((y * nw_ref[...]) * (1.0 + scf_ref[0]) + shf_ref[0]).astype(BF16)


def _merge_call(yssd, yret, proj, x2, mod3, norm_ffn_w, wbs, wbr, wo, seq, tm):
    t, d = x2.shape
    tps = seq // tm

    def tok(blk=0):
        return pl.BlockSpec((tm, d), lambda i: (i, blk))

    def modrow(k):
        return pl.BlockSpec((1, 1, d), lambda i: (i // tps, 0, k))

    return pl.pallas_call(
        _merge_kernel,
        grid=(t // tm,),
        in_specs=[tok(), tok(), tok(BLK_GA), tok(BLK_GB), tok(),
                  modrow(2), modrow(3), modrow(4), _resident((1, d)),
                  _resident(wbs.shape), _resident(wbr.shape), _resident(wo.shape)],
        out_specs=[tok(), tok()],
        out_shape=[jax.ShapeDtypeStruct((t, d), F32), jax.ShapeDtypeStruct((t, d), BF16)],
        compiler_params=pltpu.CompilerParams(
            dimension_semantics=("arbitrary",), vmem_limit_bytes=VMEM_LIMIT),
        name="merge",
    )(yssd, yret, proj, proj, x2, mod3, mod3, mod3, norm_ffn_w, wbs, wbr, wo)


def _ffn_kernel(h2_ref, x1_ref, gf_ref, nw_ref, wg_ref, wu_ref, wd_ref, o_ref):
    h2 = h2_ref[...]
    gate = _dot(h2, wg_ref[...])
    up = _dot(h2, wu_ref[...])
    down = _dot((_silu(gate) * up).astype(BF16), wd_ref[...])
    x2 = x1_ref[...] + gf_ref[0] * down
    y = x2 * lax.rsqrt(jnp.mean(x2 * x2, axis=-1, keepdims=True) + NORM_EPS)
    o_ref[...] = y * nw_ref[...]


def _ffn_call(h2, x1, mod3, norm_final_w, w_gu, wd, seq, tm):
    t, d = x1.shape
    tps = seq // tm
    hidden = wd.shape[0]

    def half(k):
        return pl.BlockSpec((d, hidden), lambda i: (0, k), pipeline_mode=pl.Buffered(1))

    return pl.pallas_call(
        _ffn_kernel,
        grid=(t // tm,),
        in_specs=[pl.BlockSpec((tm, d), lambda i: (i, 0)),
                  pl.BlockSpec((tm, d), lambda i: (i, 0)),
                  pl.BlockSpec((1, 1, d), lambda i: (i // tps, 0, 5)),
                  _resident((1, d)),
                  half(0), half(1), _resident(wd.shape)],
        out_specs=pl.BlockSpec((tm, d), lambda i: (i, 0)),
        out_shape=jax.ShapeDtypeStruct((t, d), F32),
        compiler_params=pltpu.CompilerParams(
            dimension_semantics=("arbitrary",), vmem_limit_bytes=VMEM_LIMIT),
        name="ffn",
    )(h2, x1, mod3, norm_final_w, w_gu, w_gu, wd)


def _token_tile(seq):
    tm = min(TOKEN_TILE, seq)
    assert seq % tm == 0 and tm % CHUNK == 0
    return tm


def kernel(x, c, positions, w_ada, b_ada, norm_mix_w, w_in, ssd_conv_w, ssd_conv_b, ssd_dt_bias, ssd_A_log, ssd_D, ssd_norm_w, ret_norm_w, w_branch_ssd, w_branch_ret, w_out, norm_ffn_w, w_gate_up, w_down, norm_final_w):
    bsz, seq, d = x.shape
    assert d == D_MODEL and seq % CHUNK == 0
    assert w_ada.shape[0] == 1, "single-layer block"
    t = bsz * seq
    tm = _token_tile(seq)
    x2 = x.reshape(t, d)

    mod = _mod_call(c, w_ada[0], b_ada[0])
    mod3 = mod.reshape(bsz, 1, 6 * d)

    w_z, w_xbc, w_dt, w_q, w_k, w_v, w_g, w_ga, w_gb = jnp.split(w_in[0], np.cumsum(IN_SIZES)[:-1].tolist(), axis=1)
    w_main = jnp.concatenate([w_z, w_v, w_g, w_q, w_k, w_ga, w_gb], axis=1).astype(BF16)
    w_dtp = jnp.zeros((d, LANES), BF16).at[:, :SSD_HEADS].set(w_dt.astype(BF16))

    xbc, dt_raw, proj = _inproj_calls(x2, positions, mod3, norm_mix_w.reshape(1, d), w_xbc.astype(BF16), w_dtp,
                                      w_main, ssd_conv_w[0].astype(F32), ssd_conv_b[0].reshape(1, -1).astype(F32),
                                      seq, tm)

    yssd, yret = _mixer_call(xbc, proj, dt_raw, bsz, seq, ssd_dt_bias[0], ssd_A_log[0], ssd_D[0],
                             ssd_norm_w[0], ret_norm_w[0])

    x1, h2 = _merge_call(yssd, yret, proj, x2, mod3, norm_ffn_w.reshape(1, d),
                         w_branch_ssd[0].astype(BF16), w_branch_ret[0].astype(BF16), w_out[0].astype(BF16),
                         seq, tm)

    out = _ffn_call(h2, x1, mod3, norm_final_w.reshape(1, d), w_gate_up[0].astype(BF16),
                    w_down[0].astype(BF16), seq, tm)
    return out.reshape(bsz, seq, d)
```

```python
import functools
import math

import jax
import jax.numpy as jnp
import numpy as np
from jax import lax
from jax.experimental import pallas as pl
from jax.experimental.pallas import tpu as pltpu

F32 = jnp.float32
BF16 = jnp.bfloat16

D_MODEL = 1024
SSD_HEADS = 16
SSD_HEAD_DIM = 64
SSD_D_INNER = SSD_HEADS * SSD_HEAD_DIM
SSD_GROUPS = 4
SSD_HPG = SSD_HEADS // SSD_GROUPS
SSD_D_STATE = 128
SSD_CONV = 4
SSD_BC = SSD_GROUPS * SSD_D_STATE
SSD_CONV_DIM = SSD_D_INNER + 2 * SSD_BC
RET_HEADS = 4
RET_QK_DIM = 128
RET_V_DIM = 256
RET_D_QK = RET_HEADS * RET_QK_DIM
RET_D_V = RET_HEADS * RET_V_DIM
ROPE_BASE = 10000.0
CHUNK = 128
FFN_HIDDEN = 2816
NORM_EPS = 1e-6
IN_SIZES = (SSD_D_INNER, SSD_CONV_DIM, SSD_HEADS, RET_D_QK, RET_D_QK, RET_D_V, RET_D_V, D_MODEL, D_MODEL)

LANES = 128
PROJ_W = 6144
PROJ_BLK = 1024
BLK_Z, BLK_V, BLK_G, BLK_QK, BLK_GA, BLK_GB = 0, 1, 2, 3, 4, 5
MIXER_BLOCKS = 4
CHUNKS_PER_STEP = 4
GATE_PROJ_ORDER = (BLK_QK, BLK_Z, BLK_G, BLK_GA, BLK_GB, BLK_V)
BLK_XS, BLK_BC = 0, 1
HALO = 16
SILU_BLOCKS = (BLK_Z, BLK_G)
SIGMOID_BLOCKS = (BLK_GA, BLK_GB)
LOG2E = math.log2(math.e)
MASKED_LOG2 = -1e30

VMEM_LIMIT = 56 * 1024 * 1024
TOKEN_TILE = 512
GATE_TOKEN_TILE = 1024


def _silu(v):
    return v * (1.0 / (1.0 + jnp.exp(-v)))


def _sigmoid(v):
    return 1.0 / (1.0 + jnp.exp(-v))


def _dot(a, b):
    return jnp.dot(a, b, preferred_element_type=F32)


def _dot_nt(a, b):
    return lax.dot_general(a, b, (((1,), (1,)), ((), ())), preferred_element_type=F32)


def _dot_tn(a, b):
    return lax.dot_general(a, b, (((0,), (0,)), ((), ())), preferred_element_type=F32)


def _split3(v):
    hi = v.astype(BF16)
    r1 = v - hi.astype(F32)
    mid = r1.astype(BF16)
    lo = (r1 - mid.astype(F32)).astype(BF16)
    return hi, mid, lo


def _resident(shape):
    nd = len(shape)
    return pl.BlockSpec(shape, lambda *_: (0,) * nd, pipeline_mode=pl.Buffered(1))


def _mod_kernel(c_ref, w_ref, b_ref, o_ref):
    cond = _silu(c_ref[...])
    o_ref[...] = jnp.dot(cond, w_ref[...], preferred_element_type=F32,
                         precision=lax.Precision.HIGHEST) + b_ref[...]


def _mod_call(c, w_ada, b_ada):
    bsz, d = c.shape
    n = w_ada.shape[1]
    tn = 1024
    return pl.pallas_call(
        _mod_kernel,
        grid=(n // tn,),
        in_specs=[pl.BlockSpec((bsz, d), lambda j: (0, 0)),
                  pl.BlockSpec((d, tn), lambda j: (0, j)),
                  pl.BlockSpec((1, tn), lambda j: (0, j))],
        out_specs=pl.BlockSpec((bsz, tn), lambda j: (0, j)),
        out_shape=jax.ShapeDtypeStruct((bsz, n), F32),
        name="mod",
    )(c, w_ada, b_ada.reshape(1, n))


def _modulated_norm(x, nw_ref, scale_ref, shift_ref):
    y = x * lax.rsqrt(jnp.mean(x * x, axis=-1, keepdims=True) + NORM_EPS)
    return ((y * nw_ref[...]) * (1.0 + scale_ref[0]) + shift_ref[0]).astype(BF16)


def _conv_proj_kernel(tiles_per_seq, x_ref, halo_ref, shift_ref, scale_ref, nw_ref, w_ref, wdt_ref,
                      convw_ref, convb_ref, xbc_ref, dt_ref, h_ref):
    hb = _modulated_norm(x_ref[...], nw_ref, scale_ref, shift_ref)
    h_ref[...] = hb
    hb_halo = _modulated_norm(halo_ref[...], nw_ref, scale_ref, shift_ref)
    first_tile = (pl.program_id(0) % tiles_per_seq) == 0
    dt_ref[...] = _dot(hb, wdt_ref[...])
    for blk in range(SSD_CONV_DIM // PROJ_BLK):
        cols = slice(blk * PROJ_BLK, (blk + 1) * PROJ_BLK)
        r_halo = jnp.where(first_tile, 0.0, _dot(hb_halo, w_ref[:, cols]))
        xe = jnp.concatenate([r_halo, _dot(hb, w_ref[:, cols])], axis=0)
        acc = convw_ref[0:1, cols] * xe
        for j in range(1, SSD_CONV):
            acc = convw_ref[j:j + 1, cols] * xe + pltpu.roll(acc, 1, 0)
        xbc_ref[:, cols] = _silu(acc[HALO:, :] + convb_ref[:, cols]).astype(BF16)


def _gate_proj_kernel(h_ref, pos_ref, invf_ref, w_ref, proj_ref):
    hb = h_ref[...]
    half = RET_QK_DIM // 2
    ang_t = invf_ref[...] * pos_ref[0]
    cos_t = jnp.cos(ang_t)
    sin_t = jnp.sin(ang_t)
    cos2 = jnp.concatenate([cos_t, cos_t], axis=0).T
    sin2 = jnp.concatenate([-sin_t, sin_t], axis=0).T
    kscale = RET_QK_DIM ** -0.5
    for blk in GATE_PROJ_ORDER:
        cols = slice(blk * PROJ_BLK, (blk + 1) * PROJ_BLK)
        r = _dot(hb, w_ref[:, cols])
        if blk == BLK_QK:
            for hd in range(PROJ_BLK // RET_QK_DIM):
                hcols = slice(hd * RET_QK_DIM, (hd + 1) * RET_QK_DIM)
                t_h = r[:, hcols]
                rot = t_h * cos2 + pltpu.roll(t_h, half, 1) * sin2
                if hd >= RET_HEADS:
                    rot = rot * kscale
                proj_ref[:, blk * PROJ_BLK + hd * RET_QK_DIM:blk * PROJ_BLK + (hd + 1) * RET_QK_DIM] = rot.astype(BF16)
            continue
        if blk in SILU_BLOCKS:
            r = _silu(r)
        elif blk in SIGMOID_BLOCKS:
            r = _sigmoid(r)
        proj_ref[:, cols] = r.astype(BF16)


def _inproj_calls(x2, positions, mod3, norm_w, w_xbc, w_dt, w_main, conv_w, conv_b, seq, tm):
    t, d = x2.shape
    tps = seq // tm
    halo_blocks_per_tile = tm // HALO
    params = pltpu.CompilerParams(dimension_semantics=("arbitrary",), vmem_limit_bytes=VMEM_LIMIT)
    tok = pl.BlockSpec((tm, d), lambda i: (i, 0))
    shift_m = pl.BlockSpec((1, 1, d), lambda i: (i // tps, 0, 0))
    scale_m = pl.BlockSpec((1, 1, d), lambda i: (i // tps, 0, 1))
    xbc, dt_raw, h = pl.pallas_call(
        functools.partial(_conv_proj_kernel, tps),
        grid=(t // tm,),
        in_specs=[tok,
                  pl.BlockSpec((HALO, d), lambda i: (jnp.maximum(i * halo_blocks_per_tile - 1, 0), 0)),
                  shift_m, scale_m, _resident((1, d)), _resident(w_xbc.shape), _resident(w_dt.shape),
                  _resident(conv_w.shape), _resident(conv_b.shape)],
        out_specs=[pl.BlockSpec((tm, SSD_CONV_DIM), lambda i: (i, 0)),
                   pl.BlockSpec((tm, LANES), lambda i: (i, 0)),
                   tok],
        out_shape=[jax.ShapeDtypeStruct((t, SSD_CONV_DIM), BF16),
                   jax.ShapeDtypeStruct((t, LANES), F32),
                   jax.ShapeDtypeStruct((t, d), BF16)],
        compiler_params=params,
        name="conv_proj",
    )(x2, x2, mod3, mod3, norm_w, w_xbc, w_dt, conv_w, conv_b)
    half = RET_QK_DIM // 2
    inv_freq = jnp.power(ROPE_BASE, -jnp.arange(half, dtype=F32) / half)
    tg = min(GATE_TOKEN_TILE, seq)
    invf = jnp.broadcast_to(inv_freq[:, None], (half, tg))
    proj = pl.pallas_call(
        _gate_proj_kernel,
        grid=(t // tg,),
        in_specs=[pl.BlockSpec((tg, d), lambda i: (i, 0)), pl.BlockSpec((1, 1, tg), lambda i: (i, 0, 0)),
                  _resident(invf.shape), _resident(w_main.shape)],
        out_specs=pl.BlockSpec((tg, PROJ_W), lambda i: (i, 0)),
        out_shape=jax.ShapeDtypeStruct((t, PROJ_W), BF16),
        compiler_params=params,
        name="gate_proj",
    )(h, positions.astype(F32).reshape(t // tg, 1, tg), invf, w_main)
    return xbc, dt_raw, proj


def _decay_terms(dt_raw, dtb_ref, alog_ref, tri_ref, hexp_ref, head_ref, wide_ref):
    L = CHUNK
    dtr = dt_raw + dtb_ref[...]
    dt = jnp.maximum(dtr, 0.0) + jnp.log1p(jnp.exp(-jnp.abs(dtr)))
    a2 = dt * (-LOG2E * jnp.exp(alog_ref[...]))
    a_hi, a_mid, a_lo = _split3(a2)
    acum3 = _dot(tri_ref[...], jnp.concatenate([a_hi, a_mid, a_lo], axis=1))
    acum = acum3[:, :LANES] + acum3[:, LANES:2 * LANES] + acum3[:, 2 * LANES:]
    a_last = acum[L - 1:L, :]
    head_ref[0] = acum
    head_ref[1] = (acum - jnp.log2(dt)).T
    per_head = jnp.concatenate([dt * jnp.exp2(a_last - acum), jnp.exp2(acum)], axis=0)
    hi = per_head.astype(BF16)
    mid = (per_head - hi.astype(F32)).astype(BF16)
    spread = _dot(jnp.concatenate([hi, mid], axis=1), hexp_ref[...])
    wide_ref[0] = spread[:L]
    wide_ref[1] = spread[L:]


def _mixer_kernel(xbc_ref, mix_ref, dt_ref, dtn_ref,
                  dtb_ref, alog_ref, dexp_ref, snw_ref, rnw_ref,
                  tri_ref, hexp_ref, dmask_ref, qdec_ref, kdec_ref, cdec_ref,
                  yssd_ref, yret_ref,
                  xbd_ref, sstate_ref, rstate_ref, dhead_ref, dwide_ref):
    L = CHUNK
    P = SSD_HEAD_DIM
    GP = SSD_HPG * P
    step = pl.program_id(0) * pl.num_programs(1) + pl.program_id(1)
    slot = step % 2

    @pl.when(pl.program_id(1) == 0)
    def _():
        xbd_ref[...] = jnp.zeros_like(xbd_ref)
        sstate_ref[...] = jnp.zeros_like(sstate_ref)
        rstate_ref[...] = jnp.zeros_like(rstate_ref)

    def decay_terms(src_ref, to_slot):
        for ci in range(CHUNKS_PER_STEP):
            _decay_terms(src_ref[ci * L:(ci + 1) * L, :], dtb_ref, alog_ref, tri_ref, hexp_ref,
                         dhead_ref.at[to_slot, ci], dwide_ref.at[to_slot, ci])

    @pl.when(step == 0)
    def _():
        decay_terms(dt_ref, 0)

    row = lax.broadcasted_iota(jnp.int32, (L, L), 0)
    col = lax.broadcasted_iota(jnp.int32, (L, L), 1)
    causal = row >= col

    for ci in range(CHUNKS_PER_STEP):
        rows = slice(ci * L, (ci + 1) * L)
        xs_ref = xbc_ref.at[rows, BLK_XS * PROJ_BLK:(BLK_XS + 1) * PROJ_BLK]
        bc_ref = xbc_ref.at[rows, BLK_BC * PROJ_BLK:(BLK_BC + 1) * PROJ_BLK]
        zs_ref = mix_ref.at[rows, BLK_Z * PROJ_BLK:(BLK_Z + 1) * PROJ_BLK]
        v_ref = mix_ref.at[rows, BLK_V * PROJ_BLK:(BLK_V + 1) * PROJ_BLK]
        gs_ref = mix_ref.at[rows, BLK_G * PROJ_BLK:(BLK_G + 1) * PROJ_BLK]
        qk_ref = mix_ref.at[rows, BLK_QK * PROJ_BLK:(BLK_QK + 1) * PROJ_BLK]
        acum = dhead_ref[slot, ci, 0]
        src_t = dhead_ref[slot, ci, 1]
        dtw_x = dwide_ref.at[slot, ci, 0]
        eacum_x = dwide_ref.at[slot, ci, 1]
        xs = xs_ref[...].astype(F32)
        xw = (xs * dtw_x[...]).astype(BF16)
        dskip = dexp_ref[...] * xs

        for g in range(SSD_GROUPS):
            gcols = slice(g * GP, (g + 1) * GP)
            bg = bc_ref[:, g * SSD_D_STATE:(g + 1) * SSD_D_STATE]
            cg = bc_ref[:, SSD_BC + g * SSD_D_STATE:SSD_BC + (g + 1) * SSD_D_STATE]
            cb = _dot_nt(cg, bg)
            ms = []
            for r in range(SSD_HPG):
                h = g * SSD_HPG + r
                seg = acum[:, h:h + 1] - src_t[h:h + 1, :]
                ms.append((cb * jnp.exp2(jnp.where(causal, seg, MASKED_LOG2))).astype(BF16))
                xbd_ref[ci, g, r * L:(r + 1) * L, r * P:(r + 1) * P] = xs_ref[:, g * GP + r * P:g * GP + (r + 1) * P]
            y = _dot(jnp.concatenate(ms, axis=1), xbd_ref[ci, g])
            y = y + _dot(cg, sstate_ref[g].astype(BF16)) * eacum_x[:, gcols]
            sstate_ref[g] = sstate_ref[g] * eacum_x[L - 1:L, gcols] + _dot_tn(bg, xw[:, gcols])
            y = (y + dskip[:, gcols]) * zs_ref[:, gcols].astype(F32)
            y = y * lax.rsqrt(jnp.mean(y * y, axis=-1, keepdims=True) + NORM_EPS)
            yssd_ref[rows, gcols] = (y * snw_ref[:, gcols]).astype(BF16)

        for h in range(RET_HEADS):
            qb = qk_ref[:, h * RET_QK_DIM:(h + 1) * RET_QK_DIM]
            kb = qk_ref[:, RET_D_QK + h * RET_QK_DIM:RET_D_QK + (h + 1) * RET_QK_DIM]
            vs = slice(h * RET_V_DIM, (h + 1) * RET_V_DIM)
            vh = v_ref[:, vs]
            scores = _dot_nt(qb, kb) * dmask_ref[h]
            y = _dot(scores.astype(BF16), vh)
            y = y + _dot(qb, rstate_ref[:, vs].astype(BF16)) * qdec_ref[:, vs]
            kd = (kb.astype(F32) * kdec_ref[:, h * RET_QK_DIM:(h + 1) * RET_QK_DIM]).astype(BF16)
            rstate_ref[:, vs] = rstate_ref[:, vs] * cdec_ref[:, vs] + _dot_tn(kd, vh)
            y = y * lax.rsqrt(jnp.mean(y * y, axis=-1, keepdims=True) + NORM_EPS)
            yret_ref[rows, vs] = (y * rnw_ref[:, vs] * gs_ref[:, vs].astype(F32)).astype(BF16)

    decay_terms(dtn_ref, 1 - slot)


def _mixer_consts():
    L = CHUNK
    idx = np.arange(L)
    tri = (idx[:, None] >= idx[None, :]).astype(np.float32)
    hexp = np.zeros((2 * LANES, SSD_D_INNER), np.float32)
    for piece in range(2):
        for h in range(SSD_HEADS):
            hexp[piece * LANES + h, h * SSD_HEAD_DIM:(h + 1) * SSD_HEAD_DIM] = 1.0
    return jnp.asarray(tri, BF16), jnp.asarray(hexp, BF16)


def _retention_tables():
    L = CHUNK
    H = RET_HEADS
    log_gamma = jnp.log1p(-jnp.exp2(-5.0 - jnp.arange(H, dtype=F32)))
    idx = jnp.arange(L, dtype=F32)
    diff = idx[:, None] - idx[None, :]
    dmask = jnp.where(diff >= 0, jnp.exp(jnp.maximum(diff, 0.0)[None] * log_gamma[:, None, None]), 0.0)
    q_decay = jnp.exp((idx + 1.0)[:, None] * log_gamma)
    k_decay = jnp.exp((L - 1.0 - idx)[:, None] * log_gamma)
    chunk_decay = jnp.exp(L * log_gamma)
    qdec = jnp.repeat(q_decay, RET_V_DIM, axis=1)
    kdec = jnp.repeat(k_decay, RET_QK_DIM, axis=1)
    cdec = jnp.repeat(chunk_decay, RET_V_DIM)[None, :]
    return dmask, qdec, kdec, cdec


def _mixer_call(xbc, proj, dt_raw, bsz, seq, dt_bias, a_log, d_skip, ssd_norm_w, ret_norm_w):
    t = proj.shape[0]
    rows = CHUNKS_PER_STEP * CHUNK
    assert seq % rows == 0
    nc = seq // rows
    tri, hexp = _mixer_consts()
    dmask, qdec, kdec, cdec = _retention_tables()

    def pad_heads(v):
        return jnp.zeros((1, LANES), F32).at[0, :SSD_HEADS].set(v.astype(F32))

    def tok(width):
        return pl.BlockSpec((rows, width), lambda b, c: (b * nc + c, 0))

    consts = [pad_heads(dt_bias), pad_heads(a_log),
              jnp.repeat(d_skip.astype(F32), SSD_HEAD_DIM)[None, :], ssd_norm_w.reshape(1, -1).astype(F32),
              ret_norm_w.reshape(1, -1).astype(F32), tri, hexp, dmask, qdec, kdec, cdec]
    in_specs = [tok(SSD_CONV_DIM), tok(MIXER_BLOCKS * PROJ_BLK), tok(LANES),
                pl.BlockSpec((rows, LANES), lambda b, c: (jnp.minimum(b * nc + c + 1, bsz * nc - 1), 0))]
    in_specs += [_resident(v.shape) for v in consts]
    return pl.pallas_call(
        _mixer_kernel,
        grid=(bsz, nc),
        in_specs=in_specs,
        out_specs=[tok(SSD_D_INNER), tok(RET_D_V)],
        out_shape=[jax.ShapeDtypeStruct((t, SSD_D_INNER), BF16),
                   jax.ShapeDtypeStruct((t, RET_D_V), BF16)],
        scratch_shapes=[pltpu.VMEM((CHUNKS_PER_STEP, SSD_GROUPS, SSD_HPG * CHUNK, SSD_HPG * SSD_HEAD_DIM), BF16),
                        pltpu.VMEM((SSD_GROUPS, SSD_D_STATE, SSD_HPG * SSD_HEAD_DIM), F32),
                        pltpu.VMEM((RET_QK_DIM, RET_D_V), F32),
                        pltpu.VMEM((2, CHUNKS_PER_STEP, 2, CHUNK, LANES), F32),
                        pltpu.VMEM((2, CHUNKS_PER_STEP, 2, CHUNK, SSD_D_INNER), F32)],
        compiler_params=pltpu.CompilerParams(
            dimension_semantics=("arbitrary", "arbitrary"), vmem_limit_bytes=VMEM_LIMIT),
        name="mixers",
    )(xbc, proj, dt_raw, dt_raw, *consts)


def _merge_kernel(ys_ref, yr_ref, ga_ref, gb_ref, x_ref, gm_ref, shf_ref, scf_ref, nw_ref,
                  wbs_ref, wbr_ref, wo_ref, x1_ref, h2_ref):
    a = _dot(ys_ref[...], wbs_ref[...])
    b = _dot(yr_ref[...], wbr_ref[...])
    merged = ga_ref[...].astype(F32) * a + gb_ref[...].astype(F32) * b
    x1 = x_ref[...] + gm_ref[0] * _dot(merged.astype(BF16), wo_ref[...])
    x1_ref[...] = x1
    y = x1 * lax.rsqrt(jnp.mean(x1 * x1, axis=-1, keepdims=True) + NORM_EPS)
    h2_ref[...] = ((y * nw_ref[...]) * (1.0 + scf_ref[0]) + shf_ref[0]).astype(BF16)


def _merge_call(yssd, yret, proj, x2, mod3, norm_ffn_w, wbs, wbr, wo, seq, tm):
    t, d = x2.shape
    tps = seq // tm

    def tok(blk=0):
        return pl.BlockSpec((tm, d), lambda i: (i, blk))

    def modrow(k):
        return pl.BlockSpec((1, 1, d), lambda i: (i // tps, 0, k))

    return pl.pallas_call(
        _merge_kernel,
        grid=(t // tm,),
        in_specs=[tok(), tok(), tok(BLK_GA), tok(BLK_GB), tok(),
                  modrow(2), modrow(3), modrow(4), _resident((1, d)),
                  _resident(wbs.shape), _resident(wbr.shape), _resident(wo.shape)],
        out_specs=[tok(), tok()],
        out_shape=[jax.ShapeDtypeStruct((t, d), F32), jax.ShapeDtypeStruct((t, d), BF16)],
        compiler_params=pltpu.CompilerParams(
            dimension_semantics=("arbitrary",), vmem_limit_bytes=VMEM_LIMIT),
        name="merge",
    )(yssd, yret, proj, proj, x2, mod3, mod3, mod3, norm_ffn_w, wbs, wbr, wo)


def _ffn_kernel(h2_ref, x1_ref, gf_ref, nw_ref, wg_ref, wu_ref, wd_ref, o_ref):
    h2 = h2_ref[...]
    gate = _dot(h2, wg_ref[...])
    up = _dot(h2, wu_ref[...])
    down = _dot((_silu(gate) * up).astype(BF16), wd_ref[...])
    x2 = x1_ref[...] + gf_ref[0] * down
    y = x2 * lax.rsqrt(jnp.mean(x2 * x2, axis=-1, keepdims=True) + NORM_EPS)
    o_ref[...] = y * nw_ref[...]


def _ffn_call(h2, x1, mod3, norm_final_w, w_gu, wd, seq, tm):
    t, d = x1.shape
    tps = seq // tm
    hidden = wd.shape[0]

    def half(k):
        return pl.BlockSpec((d, hidden), lambda i: (0, k), pipeline_mode=pl.Buffered(1))

    return pl.pallas_call(
        _ffn_kernel,
        grid=(t // tm,),
        in_specs=[pl.BlockSpec((tm, d), lambda i: (i, 0)),
                  pl.BlockSpec((tm, d), lambda i: (i, 0)),
                  pl.BlockSpec((1, 1, d), lambda i: (i // tps, 0, 5)),
                  _resident((1, d)),
                  half(0), half(1), _resident(wd.shape)],
        out_specs=pl.BlockSpec((tm, d), lambda i: (i, 0)),
        out_shape=jax.ShapeDtypeStruct((t, d), F32),
        compiler_params=pltpu.CompilerParams(
            dimension_semantics=("arbitrary",), vmem_limit_bytes=VMEM_LIMIT),
        name="ffn",
    )(h2, x1, mod3, norm_final_w, w_gu, w_gu, wd)


def _token_tile(seq):
    tm = min(TOKEN_TILE, seq)
    assert seq % tm == 0 and tm % CHUNK == 0
    return tm


def kernel(x, c, positions, w_ada, b_ada, norm_mix_w, w_in, ssd_conv_w, ssd_conv_b, ssd_dt_bias, ssd_A_log, ssd_D, ssd_norm_w, ret_norm_w, w_branch_ssd, w_branch_ret, w_out, norm_ffn_w, w_gate_up, w_down, norm_final_w):
    bsz, seq, d = x.shape
    assert d == D_MODEL and seq % CHUNK == 0
    assert w_ada.shape[0] == 1, "single-layer block"
    t = bsz * seq
    tm = _token_tile(seq)
    x2 = x.reshape(t, d)

    mod = _mod_call(c, w_ada[0], b_ada[0])
    mod3 = mod.reshape(bsz, 1, 6 * d)

    w_z, w_xbc, w_dt, w_q, w_k, w_v, w_g, w_ga, w_gb = jnp.split(w_in[0], np.cumsum(IN_SIZES)[:-1].tolist(), axis=1)
    w_main = jnp.concatenate([w_z, w_v, w_g, w_q, w_k, w_ga, w_gb], axis=1).astype(BF16)
    w_dtp = jnp.zeros((d, LANES), BF16).at[:, :SSD_HEADS].set(w_dt.astype(BF16))

    xbc, dt_raw, proj = _inproj_calls(x2, positions, mod3, norm_mix_w.reshape(1, d), w_xbc.astype(BF16), w_dtp,
                                      w_main, ssd_conv_w[0].astype(F32), ssd_conv_b[0].reshape(1, -1).astype(F32),
                                      seq, tm)

    yssd, yret = _mixer_call(xbc, proj, dt_raw, bsz, seq, ssd_dt_bias[0], ssd_A_log[0], ssd_D[0],
                             ssd_norm_w[0], ret_norm_w[0])

    x1, h2 = _merge_call(yssd, yret, proj, x2, mod3, norm_ffn_w.reshape(1, d),
                         w_branch_ssd[0].astype(BF16), w_branch_ret[0].astype(BF16), w_out[0].astype(BF16),
                         seq, min(GATE_TOKEN_TILE, seq))

    out = _ffn_call(h2, x1, mod3, norm_final_w.reshape(1, d), w_gate_up[0].astype(BF16),
                    w_down[0].astype(BF16), seq, tm)
    return out.reshape(bsz, seq, d)
```
